```python
import jax, jax.numpy as jnp
from jax import lax
import numpy as np

D_MODEL = 1024
BATCH = 8
SEQ = 4096
DEPTH = 4

N_MIXERS = 3
N_SSD_LAYERS = (DEPTH + 2) // 3
N_CONV_LAYERS = (DEPTH + 1) // 3
N_ATT_LAYERS = DEPTH // 3
DEEPNORM_ALPHA = (2 * DEPTH) ** 0.25
DEEPNORM_BETA = (8 * DEPTH) ** -0.25
LN_EPS = 1e-5

SSD_D_INNER = 2 * D_MODEL
SSD_HEAD_DIM = 64
SSD_N_HEADS = SSD_D_INNER // SSD_HEAD_DIM
SSD_N_GROUPS = 4
SSD_D_STATE = 128
SSD_CONV_WIDTH = 4
SSD_CHUNK = 256
SSD_CONV_DIM = SSD_D_INNER + 2 * SSD_N_GROUPS * SSD_D_STATE
SSD_IN_COLS = SSD_D_INNER + SSD_CONV_DIM + SSD_N_HEADS

SC_WIDTH = 3

ATT_GROUPS = ((128, 1), (512, 4), (2048, 16))
N_ATT_GROUPS = len(ATT_GROUPS)
ATT_HEAD_DIM = 64
ATT_HEADS = D_MODEL // ATT_HEAD_DIM
ATT_BLOCK = 128
ATT_QKV_COLS = N_ATT_GROUPS * 3 * ATT_HEADS * ATT_HEAD_DIM

MOE_GROUPS = 4
MOE_EXPERTS_PER_GROUP = 8
MOE_N_EXPERTS = MOE_GROUPS * MOE_EXPERTS_PER_GROUP
MOE_TOP_K = 2
MOE_D_EXPERT = 256

kernel_name = 'hybrid_ssd_shortconv_dilattn_hmoe_deepnorm'


def layernorm(x, g, b):
    xf = x.astype(jnp.float32)
    mu = jnp.mean(xf, axis=-1, keepdims=True)
    var = jnp.mean(jnp.square(xf - mu), axis=-1, keepdims=True)
    return ((xf - mu) * lax.rsqrt(var + LN_EPS) * g.astype(jnp.float32) + b.astype(jnp.float32)).astype(x.dtype)


def causal_depthwise_conv(u, w):
    k, c = w.shape
    return lax.conv_general_dilated(u, w[:, None, :].astype(u.dtype), window_strides=(1,),
                                    padding=[(k - 1, 0)], dimension_numbers=('NWC', 'WIO', 'NWC'),
                                    feature_group_count=c)


def ssd_chunked_scan(xh, dt, a, bm, cm):
    bsz, s = xh.shape[:2]
    g, j = SSD_N_GROUPS, SSD_N_HEADS // SSD_N_GROUPS
    L = SSD_CHUNK
    s_pad = -(-s // L) * L
    nc = s_pad // L

    def chunks(u):
        u = jnp.pad(u, [(0, 0), (0, s_pad - s)] + [(0, 0)] * (u.ndim - 2))
        return jnp.moveaxis(u.reshape((bsz, nc, L) + u.shape[2:]), 1, 0)

    xc = chunks(xh.reshape(bsz, s, g, j, SSD_HEAD_DIM))
    dtc = chunks(dt.reshape(bsz, s, g, j))
    bc, cc = chunks(bm), chunks(cm)
    a_gj = a.reshape(g, j)
    causal = jnp.tril(jnp.ones((L, L), dtype=bool))

    def step(state, inp):
        x_k, dt_k, b_k, c_k = inp
        cs = jnp.cumsum(dt_k * a_gj, axis=1)
        seg = cs[:, :, None] - cs[:, None, :]
        decay = jnp.exp(jnp.where(causal[None, :, :, None, None], seg, -jnp.inf))
        cb = jnp.einsum('btgn,bsgn->btsg', c_k, b_k)
        mix = cb[..., None] * decay * dt_k[:, None]
        y_diag = jnp.einsum('btsgj,bsgjp->btgjp', mix, x_k)
        y_off = jnp.einsum('btgn,bgjpn->btgjp', c_k, state) * jnp.exp(cs)[..., None]
        last = cs[:, -1]
        w_s = jnp.exp(last[:, None] - cs) * dt_k
        state = state * jnp.exp(last)[..., None, None] + jnp.einsum('bsgn,bsgj,bsgjp->bgjpn', b_k, w_s, x_k)
        return state, y_diag + y_off

    state0 = jnp.zeros((bsz, g, j, SSD_HEAD_DIM, SSD_D_STATE), jnp.float32)
    _, y = lax.scan(step, state0, (xc, dtc, bc, cc))
    return jnp.moveaxis(y, 0, 1).reshape(bsz, s_pad, SSD_N_HEADS, SSD_HEAD_DIM)[:, :s]


def ssd_mixer(x, w_in, conv_w, conv_b, dt_bias, a_log, d_skip, norm_w, w_out):
    bsz, s, _ = x.shape
    zxbcdt = x @ w_in
    z, xbc, dt = jnp.split(zxbcdt, [SSD_D_INNER, SSD_D_INNER + SSD_CONV_DIM], axis=-1)
    xbc = jax.nn.silu(causal_depthwise_conv(xbc, conv_w) + conv_b)
    xs, bm, cm = jnp.split(xbc, [SSD_D_INNER, SSD_D_INNER + SSD_N_GROUPS * SSD_D_STATE], axis=-1)
    xh = xs.reshape(bsz, s, SSD_N_HEADS, SSD_HEAD_DIM).astype(jnp.float32)
    dtf = jax.nn.softplus(dt.astype(jnp.float32) + dt_bias.astype(jnp.float32))
    a = -jnp.exp(a_log.astype(jnp.float32))
    bm = bm.reshape(bsz, s, SSD_N_GROUPS, SSD_D_STATE).astype(jnp.float32)
    cm = cm.reshape(bsz, s, SSD_N_GROUPS, SSD_D_STATE).astype(jnp.float32)
    y = ssd_chunked_scan(xh, dtf, a, bm, cm) + d_skip.astype(jnp.float32)[:, None] * xh
    y = y.reshape(bsz, s, SSD_D_INNER) * jax.nn.silu(z.astype(jnp.float32))
    yg = y.reshape(bsz, s, SSD_N_GROUPS, SSD_D_INNER // SSD_N_GROUPS)
    yg = yg * lax.rsqrt(jnp.mean(jnp.square(yg), axis=-1, keepdims=True) + LN_EPS)
    y = (yg.reshape(bsz, s, SSD_D_INNER) * norm_w.astype(jnp.float32)).astype(x.dtype)
    return y @ w_out


def short_conv_mixer(x, w_in, conv_w, w_out):
    gb, gc, u = jnp.split(x @ w_in, 3, axis=-1)
    return (gb * causal_depthwise_conv(gc * u, conv_w)) @ w_out


def dilated_window_attention(q, k, v, window, dil, slopes):
    bsz, s, h, hd = q.shape
    span = dil * ATT_BLOCK
    s_pad = -(-s // span) * span
    nb = s_pad // span
    pad = [(0, 0), (0, s_pad - s), (0, 0), (0, 0)]
    qb, kb, vb = (jnp.pad(u, pad).reshape(bsz, nb, ATT_BLOCK, dil, h, hd) for u in (q, k, v))

    def with_prev(u):
        prev = jnp.concatenate([jnp.zeros_like(u[:, :1]), u[:, :-1]], axis=1)
        return jnp.concatenate([prev, u], axis=2)

    kk, vv = with_prev(kb), with_prev(vb)
    sc = jnp.einsum('bnqrhd,bnkrhd->bnrhqk', qb, kk).astype(jnp.float32) * (hd ** -0.5)
    qi = jnp.arange(ATT_BLOCK)[:, None]
    kj = jnp.arange(2 * ATT_BLOCK)[None, :]
    dist = qi + ATT_BLOCK - kj
    valid = (dist >= 0) & (dist <= window // dil)
    has_prev = (jnp.arange(nb)[:, None, None] > 0) | (kj[None] >= ATT_BLOCK)
    mask = valid[None] & has_prev
    alibi = slopes[:, None, None] * (dist * dil).astype(jnp.float32)[None]
    sc = jnp.where(mask[None, :, None, None], sc - alibi[None, None, None], -jnp.inf)
    m = jnp.max(sc, axis=-1, keepdims=True)
    p = jnp.exp(sc - m)
    denom = jnp.sum(p, axis=-1, keepdims=True)
    o = jnp.einsum('bnrhqk,bnkrhd->bnqrhd', (p / denom).astype(v.dtype), vv)
    lse = jnp.transpose((m + jnp.log(denom))[..., 0], (0, 1, 4, 2, 3))
    return o.reshape(bsz, s_pad, h, hd)[:, :s], lse.reshape(bsz, s_pad, h)[:, :s]


def dilated_attention_mixer(x, w_qkv, w_out):
    bsz, s, _ = x.shape
    qkv = (x @ w_qkv).reshape(bsz, s, N_ATT_GROUPS, 3, ATT_HEADS, ATT_HEAD_DIM)
    slopes = jnp.asarray(2.0 ** (-8.0 * np.arange(1, ATT_HEADS + 1) / ATT_HEADS), dtype=jnp.float32)
    outs, lses = [], []
    for gi, (window, dil) in enumerate(ATT_GROUPS):
        o, l = dilated_window_attention(qkv[:, :, gi, 0], qkv[:, :, gi, 1], qkv[:, :, gi, 2], window, dil, slopes)
        outs.append(o)
        lses.append(l)
    wts = jax.nn.softmax(jnp.stack(lses), axis=0)
    o = jnp.einsum('gbsh,gbshd->bshd', wts, jnp.stack(outs).astype(jnp.float32)).astype(x.dtype)
    return o.reshape(bsz, s, ATT_HEADS * ATT_HEAD_DIM) @ w_out


def hierarchical_moe(x, wg, bg, we, be, w_gate, w_up, w_down):
    bsz, s, d = x.shape
    t = x.reshape(-1, d)
    n_tok = t.shape[0]
    g_logits = (t @ wg + bg).astype(jnp.float32)
    g_prob = jax.nn.softmax(g_logits, axis=-1)
    g_sel = jnp.argmax(g_logits, axis=-1)
    g_w = jnp.take_along_axis(g_prob, g_sel[:, None], axis=-1)
    e_logits = (t @ we + be).astype(jnp.float32).reshape(n_tok, MOE_GROUPS, MOE_EXPERTS_PER_GROUP)
    e_in = jnp.take_along_axis(e_logits, g_sel[:, None, None], axis=1)[:, 0]
    top_v, top_i = lax.top_k(e_in, MOE_TOP_K)
    w2 = jax.nn.softmax(top_v, axis=-1) * g_w
    idx = g_sel[:, None] * MOE_EXPERTS_PER_GROUP + top_i
    combine = jnp.sum(jax.nn.one_hot(idx, MOE_N_EXPERTS, dtype=jnp.float32) * w2[..., None], axis=1)

    def body(acc, p):
        wg_e, wu_e, wd_e, c_e = p
        hcur = jax.nn.silu(t @ wg_e) * (t @ wu_e)
        return acc + ((c_e[:, None].astype(hcur.dtype) * hcur) @ wd_e).astype(jnp.float32), None

    acc0 = jnp.zeros((n_tok, d), jnp.float32)
    y, _ = lax.scan(body, acc0, (w_gate, w_up, w_down, combine.T))
    return y.astype(x.dtype).reshape(bsz, s, d)


def setup_inputs(seed: int = 0) -> dict:
    key = jax.random.key(seed)
    ks = jax.random.split(key, 24)
    nrm = jax.random.normal
    f32 = jnp.float32
    dt0 = jnp.exp(jax.random.uniform(ks[5], (N_SSD_LAYERS, SSD_N_HEADS), f32, np.log(1e-3), np.log(1e-1)))
    return {
        'x': nrm(ks[0], (BATCH, SEQ, D_MODEL), f32),
        'ssd_w_in': nrm(ks[1], (N_SSD_LAYERS, D_MODEL, SSD_IN_COLS), f32) * D_MODEL ** -0.5,
        'ssd_conv_w': nrm(ks[2], (N_SSD_LAYERS, SSD_CONV_WIDTH, SSD_CONV_DIM), f32) * SSD_CONV_WIDTH ** -0.5,
        'ssd_conv_b': 0.01 * nrm(ks[3], (N_SSD_LAYERS, SSD_CONV_DIM), f32),
        'ssd_dt_bias': dt0 + jnp.log(-jnp.expm1(-dt0)),
        'ssd_a_log': jnp.log(jax.random.uniform(ks[4], (N_SSD_LAYERS, SSD_N_HEADS), f32, 1.0, 16.0)),
        'ssd_d': 1.0 + 0.1 * nrm(ks[6], (N_SSD_LAYERS, SSD_N_HEADS), f32),
        'ssd_norm_w': 1.0 + 0.1 * nrm(ks[7], (N_SSD_LAYERS, SSD_D_INNER), f32),
        'ssd_w_out': nrm(ks[8], (N_SSD_LAYERS, SSD_D_INNER, D_MODEL), f32) * SSD_D_INNER ** -0.5 * DEEPNORM_BETA,
        'sc_w_in': nrm(ks[9], (N_CONV_LAYERS, D_MODEL, 3 * D_MODEL), f32) * D_MODEL ** -0.5,
        'sc_conv_w': nrm(ks[10], (N_CONV_LAYERS, SC_WIDTH, D_MODEL), f32) * SC_WIDTH ** -0.5,
        'sc_w_out': nrm(ks[11], (N_CONV_LAYERS, D_MODEL, D_MODEL), f32) * D_MODEL ** -0.5 * DEEPNORM_BETA,
        'att_w_qkv': nrm(ks[12], (N_ATT_LAYERS, D_MODEL, ATT_QKV_COLS), f32) * D_MODEL ** -0.5,
        'att_w_out': nrm(ks[13], (N_ATT_LAYERS, ATT_HEADS * ATT_HEAD_DIM, D_MODEL), f32) * (ATT_HEADS * ATT_HEAD_DIM) ** -0.5 * DEEPNORM_BETA,
        'moe_wg': nrm(ks[14], (DEPTH, D_MODEL, MOE_GROUPS), f32) * D_MODEL ** -0.5,
        'moe_bg': 0.01 * nrm(ks[15], (DEPTH, MOE_GROUPS), f32),
        'moe_we': nrm(ks[16], (DEPTH, D_MODEL, MOE_N_EXPERTS), f32) * D_MODEL ** -0.5,
        'moe_be': 0.01 * nrm(ks[17], (DEPTH, MOE_N_EXPERTS), f32),
        'moe_w_gate': nrm(ks[18], (DEPTH, MOE_N_EXPERTS, D_MODEL, MOE_D_EXPERT), f32) * D_MODEL ** -0.5,
        'moe_w_up': nrm(ks[19], (DEPTH, MOE_N_EXPERTS, D_MODEL, MOE_D_EXPERT), f32) * D_MODEL ** -0.5,
        'moe_w_down': nrm(ks[20], (DEPTH, MOE_N_EXPERTS, MOE_D_EXPERT, D_MODEL), f32) * MOE_D_EXPERT ** -0.5 * DEEPNORM_BETA,
        'ln_g': 1.0 + 0.1 * nrm(ks[21], (DEPTH, 2, D_MODEL), f32),
        'ln_b': 0.01 * nrm(ks[22], (DEPTH, 2, D_MODEL), f32),
    }


def reference(x, ssd_w_in, ssd_conv_w, ssd_conv_b, ssd_dt_bias, ssd_a_log, ssd_d, ssd_norm_w, ssd_w_out,
              sc_w_in, sc_conv_w, sc_w_out, att_w_qkv, att_w_out,
              moe_wg, moe_bg, moe_we, moe_be, moe_w_gate, moe_w_up, moe_w_down, ln_g, ln_b):
    for i in range(DEPTH):
        kind, j = i % N_MIXERS, i // N_MIXERS
        if kind == 0:
            h = ssd_mixer(x, ssd_w_in[j], ssd_conv_w[j], ssd_conv_b[j], ssd_dt_bias[j], ssd_a_log[j],
                          ssd_d[j], ssd_norm_w[j], ssd_w_out[j])
        elif kind == 1:
            h = short_conv_mixer(x, sc_w_in[j], sc_conv_w[j], sc_w_out[j])
        else:
            h = dilated_attention_mixer(x, att_w_qkv[j], att_w_out[j])
        x = layernorm(DEEPNORM_ALPHA * x + h, ln_g[i, 0], ln_b[i, 0])
        f = hierarchical_moe(x, moe_wg[i], moe_bg[i], moe_we[i], moe_be[i], moe_w_gate[i], moe_w_up[i], moe_w_down[i])
        x = layernorm(DEEPNORM_ALPHA * x + f, ln_g[i, 1], ln_b[i, 1])
    return x
```

```python
import functools
import math

import numpy as np
import jax
import jax.numpy as jnp
from jax import lax
from jax.experimental import pallas as pl
from jax.experimental.pallas import tpu as pltpu

D_MODEL = 1024
DEPTH = 4
N_MIXERS = 3
DEEPNORM_ALPHA = (2 * DEPTH) ** 0.25
LN_EPS = 1e-5

SSD_D_INNER = 2 * D_MODEL
SSD_HEAD_DIM = 64
SSD_N_HEADS = SSD_D_INNER // SSD_HEAD_DIM
SSD_N_GROUPS = 4
SSD_D_STATE = 128
SSD_CONV_WIDTH = 4
SSD_CHUNK = 256
SSD_CONV_DIM = SSD_D_INNER + 2 * SSD_N_GROUPS * SSD_D_STATE
SSD_HEADS_PER_GROUP = SSD_N_HEADS // SSD_N_GROUPS
SSD_GROUP_WIDTH = SSD_D_INNER // SSD_N_GROUPS

SC_WIDTH = 3

ATT_GROUPS = ((128, 1), (512, 4), (2048, 16))
ATT_HEAD_DIM = 64
ATT_HEADS = D_MODEL // ATT_HEAD_DIM
ATT_BLOCK = 128
ATT_COLS = ATT_HEADS * ATT_HEAD_DIM

MOE_GROUPS = 4
MOE_EXPERTS_PER_GROUP = 8
MOE_N_EXPERTS = MOE_GROUPS * MOE_EXPERTS_PER_GROUP
MOE_TOP_K = 2
MOE_D_EXPERT = 256

LANES = 128
SUBLANES = 8
VMEM_LIMIT_BYTES = 56 * 1024 * 1024

PROJ_TM = 512
CONV_TM = 512
MERGE_TM = 512
ROUTE_TM = 512
DISP_TM = 512
COMB_TM = 256
FFN_TM = 256
COL_CHUNK = 512
ROW_TILE = D_MODEL // LANES
assert ROW_TILE == SUBLANES

F32 = jnp.float32
BF16 = jnp.bfloat16
HIGHEST = lax.Precision.HIGHEST


def _cparams(n_axes):
    return pltpu.CompilerParams(dimension_semantics=("arbitrary",) * n_axes,
                                vmem_limit_bytes=VMEM_LIMIT_BYTES)


def _resident(shape):
    nd = len(shape)
    return pl.BlockSpec(shape, lambda *_: (0,) * nd, pipeline_mode=pl.Buffered(1))


def _silu(v):
    return v * (1.0 / (1.0 + jnp.exp(-v)))


def _softplus(v):
    return jnp.maximum(v, 0.0) + jnp.log1p(jnp.exp(-jnp.abs(v)))


def _layernorm(v, g, b):
    mu = jnp.mean(v, axis=-1, keepdims=True)
    d = v - mu
    var = jnp.mean(d * d, axis=-1, keepdims=True)
    return d * lax.rsqrt(var + LN_EPS) * g + b


def _split_hi_lo(v):
    hi = v.astype(BF16)
    lo = (v - hi.astype(F32)).astype(BF16)
    return jnp.concatenate([hi, lo], axis=1)


def _expansion_matrix(n_heads, width):
    e = np.zeros((2 * LANES, n_heads * width), np.float32)
    for h in range(n_heads):
        e[h, h * width:(h + 1) * width] = 1.0
        e[LANES + h, h * width:(h + 1) * width] = 1.0
    return jnp.asarray(e, dtype=BF16)


def _to_row_tiles(dst_ref, v, rows):
    for c in range(ROW_TILE):
        dst_ref[pl.ds(c, rows, stride=ROW_TILE), :] = v[:, c * LANES:(c + 1) * LANES]


def _from_row_tiles(src_ref, rows):
    return jnp.concatenate(
        [src_ref[pl.ds(c, rows, stride=ROW_TILE), :] for c in range(ROW_TILE)], axis=1)


def _proj_kernel(*refs, widths, has_exact):
    x_ref, w_ref = refs[0], refs[1]
    pos = 2
    wx_ref = None
    if has_exact:
        wx_ref = refs[pos]
        pos += 1
    outs = refs[pos:]
    x = x_ref[...]
    xb = x.astype(BF16)
    off = 0
    for j, n in enumerate(widths):
        for c in range(n // COL_CHUNK):
            r = jnp.dot(xb, w_ref[:, off + c * COL_CHUNK: off + (c + 1) * COL_CHUNK],
                        preferred_element_type=F32)
            outs[j][:, c * COL_CHUNK:(c + 1) * COL_CHUNK] = r.astype(outs[j].dtype)
        off += n
    if has_exact:
        outs[len(widths)][...] = jnp.dot(x, wx_ref[...], precision=HIGHEST,
                                         preferred_element_type=F32)


def _project(x, w_bf16, widths, w_exact=None, name="proj"):
    t, k = x.shape
    n_total = sum(widths)
    assert w_bf16.shape == (k, n_total) and t % PROJ_TM == 0
    in_specs = [pl.BlockSpec((PROJ_TM, k), lambda i: (i, 0)), _resident((k, n_total))]
    args = [x, w_bf16]
    out_shape = [jax.ShapeDtypeStruct((t, n), BF16) for n in widths]
    out_specs = [pl.BlockSpec((PROJ_TM, n), lambda i: (i, 0)) for n in widths]
    if w_exact is not None:
        in_specs.append(_resident(w_exact.shape))
        args.append(w_exact)
        out_shape.append(jax.ShapeDtypeStruct((t, w_exact.shape[1]), F32))
        out_specs.append(pl.BlockSpec((PROJ_TM, w_exact.shape[1]), lambda i: (i, 0)))
    return pl.pallas_call(
        functools.partial(_proj_kernel, widths=tuple(widths), has_exact=w_exact is not None),
        out_shape=out_shape, grid=(t // PROJ_TM,), in_specs=in_specs, out_specs=out_specs,
        compiler_params=_cparams(1), name=name)(*args)


def _ssd_kernel(z_ref, xbc_ref, dt_ref, x_ref, convw_ref, convb_ref, dtb_ref, alog_ref,
                dskip_ref, normw_ref, wout_ref, lng_ref, lnb_ref, e_ref,
                o_ref, state_ref, convbuf_ref, ydiag_ref):
    L = SSD_CHUNK
    N = SSD_D_STATE
    GW = SSD_GROUP_WIDTH
    c_idx = pl.program_id(1)

    @pl.when(c_idx == 0)
    def _():
        state_ref[...] = jnp.zeros_like(state_ref)
        convbuf_ref[0:SUBLANES, :] = jnp.zeros((SUBLANES, SSD_CONV_DIM), F32)

    u = xbc_ref[...].astype(F32)
    convbuf_ref[SUBLANES:SUBLANES + L, :] = u
    cw = convw_ref[...]
    acc = convb_ref[...] + cw[3:4, :] * u
    for k in range(SSD_CONV_WIDTH - 1):
        shift = SSD_CONV_WIDTH - 1 - k
        acc = acc + cw[k:k + 1, :] * convbuf_ref[SUBLANES - shift:SUBLANES - shift + L, :]
    convbuf_ref[0:SUBLANES, :] = convbuf_ref[L:L + SUBLANES, :]
    act = _silu(acc)
    xs = act[:, :SSD_D_INNER]
    xs_b = xs.astype(BF16)
    bmat = act[:, SSD_D_INNER:SSD_D_INNER + SSD_N_GROUPS * N]
    cmat = act[:, SSD_D_INNER + SSD_N_GROUPS * N:]

    dt = _softplus(dt_ref[...] + dtb_ref[...])
    a = -jnp.exp(alog_ref[...])
    dta = dt * a
    row = lax.broadcasted_iota(jnp.int32, (L, L), 0)
    col = lax.broadcasted_iota(jnp.int32, (L, L), 1)
    causal = row >= col
    cs = jnp.dot(causal.astype(F32), dta, precision=HIGHEST, preferred_element_type=F32)
    cs_t = cs.T
    dt_t = dt.T
    last = cs[L - 1:L, :]

    e_mat = e_ref[...]
    ecs_x = jnp.dot(_split_hi_lo(jnp.exp(cs)), e_mat, preferred_element_type=F32)
    ws_x = jnp.dot(_split_hi_lo(jnp.exp(last - cs) * dt), e_mat, preferred_element_type=F32)
    elast_x = jnp.dot(_split_hi_lo(jnp.broadcast_to(jnp.exp(last), (SUBLANES, LANES))), e_mat,
                      preferred_element_type=F32)[0:1, :]
    xw_b = (xs * ws_x).astype(BF16)

    lane = lax.broadcasted_iota(jnp.int32, (1, LANES), 1)
    low_half = lane < SSD_HEAD_DIM
    yoff_parts = []
    for g in range(SSD_N_GROUPS):
        bg = bmat[:, g * N:(g + 1) * N]
        cg_b = cmat[:, g * N:(g + 1) * N].astype(BF16)
        bg_b = bg.astype(BF16)
        cb = lax.dot_general(cg_b, bg_b, (((1,), (1,)), ((), ())), preferred_element_type=F32)
        s_old = state_ref[g]
        yoff_parts.append(jnp.dot(cg_b, s_old.astype(BF16), preferred_element_type=F32))
        for pr in range(SSD_HEADS_PER_GROUP // 2):
            pair = g * (SSD_HEADS_PER_GROUP // 2) + pr
            x_pair = xs_b[:, pair * LANES:(pair + 1) * LANES]
            ys = []
            for half in range(2):
                h = 2 * pair + half
                seg = cs[:, h:h + 1] - cs_t[h:h + 1, :]
                decay = jnp.exp(jnp.where(causal, seg, -jnp.inf))
                mix = cb * decay * dt_t[h:h + 1, :]
                ys.append(jnp.dot(mix.astype(BF16), x_pair, preferred_element_type=F32))
            ydiag_ref[:, pair * LANES:(pair + 1) * LANES] = jnp.where(low_half, ys[0], ys[1])
        bg_t = bg.T.astype(BF16)
        upd = jnp.dot(bg_t, xw_b[:, g * GW:(g + 1) * GW], preferred_element_type=F32)
        state_ref[g] = s_old * elast_x[:, g * GW:(g + 1) * GW] + upd

    yoff = jnp.concatenate(yoff_parts, axis=1)
    y = ydiag_ref[...] + yoff * ecs_x + dskip_ref[...] * xs
    y = y * _silu(z_ref[...].astype(F32))
    normw = normw_ref[...]
    yn_parts = []
    for g in range(SSD_N_GROUPS):
        yg = y[:, g * GW:(g + 1) * GW]
        ms = jnp.mean(yg * yg, axis=-1, keepdims=True)
        yn_parts.append((yg * lax.rsqrt(ms + LN_EPS) * normw[:, g * GW:(g + 1) * GW]).astype(BF16))
    yn = jnp.concatenate(yn_parts, axis=1)
    hproj = jnp.dot(yn, wout_ref[...], preferred_element_type=F32)
    o_ref[...] = _layernorm(DEEPNORM_ALPHA * x_ref[...] + hproj, lng_ref[...], lnb_ref[...])


def _ssd_mixer(x2d, bsz, seq, w_in, conv_w, conv_b, dt_bias, a_log, d_skip, norm_w, w_out,
               ln_g, ln_b):
    t = x2d.shape[0]
    nc = seq // SSD_CHUNK
    assert seq % SSD_CHUNK == 0
    n_main = SSD_D_INNER + SSD_CONV_DIM
    w_main = w_in[:, :n_main].astype(BF16)
    w_dt = jnp.pad(w_in[:, n_main:], ((0, 0), (0, LANES - SSD_N_HEADS)))
    z, xbc, dt_raw = _project(x2d, w_main, (SSD_D_INNER, SSD_CONV_DIM), w_exact=w_dt,
                              name="ssd_in_proj")
    pad = (0, LANES - SSD_N_HEADS)
    dtb = jnp.pad(dt_bias, pad).reshape(1, LANES)
    alog = jnp.pad(a_log, pad).reshape(1, LANES)
    dskip = jnp.repeat(d_skip, SSD_HEAD_DIM).reshape(1, SSD_D_INNER)
    L = SSD_CHUNK
    blk = lambda w: pl.BlockSpec((L, w), lambda b, c: (b * nc + c, 0))
    return pl.pallas_call(
        _ssd_kernel,
        out_shape=jax.ShapeDtypeStruct((t, D_MODEL), F32),
        grid=(bsz, nc),
        in_specs=[blk(SSD_D_INNER), blk(SSD_CONV_DIM), blk(LANES), blk(D_MODEL),
                  _resident((SSD_CONV_WIDTH, SSD_CONV_DIM)), _resident((1, SSD_CONV_DIM)),
                  _resident((1, LANES)), _resident((1, LANES)),
                  _resident((1, SSD_D_INNER)), _resident((1, SSD_D_INNER)),
                  _resident((SSD_D_INNER, D_MODEL)),
                  _resident((1, D_MODEL)), _resident((1, D_MODEL)),
                  _resident((2 * LANES, SSD_D_INNER))],
        out_specs=blk(D_MODEL),
        scratch_shapes=[pltpu.VMEM((SSD_N_GROUPS, SSD_D_STATE, SSD_GROUP_WIDTH), F32),
                        pltpu.VMEM((L + SUBLANES, SSD_CONV_DIM), F32),
                        pltpu.VMEM((L, SSD_D_INNER), F32)],
        compiler_params=_cparams(2), name="ssd_scan",
    )(z, xbc, dt_raw, x2d, conv_w, conv_b.reshape(1, -1), dtb, alog, dskip,
      norm_w.reshape(1, -1), w_out.astype(BF16), ln_g.reshape(1, -1), ln_b.reshape(1, -1),
      _expansion_matrix(SSD_N_HEADS, SSD_HEAD_DIM))


def _sconv_kernel(x_ref, win_ref, convw_ref, wout_ref, lng_ref, lnb_ref, o_ref, vbuf_ref):
    tm = CONV_TM

    @pl.when(pl.program_id(1) == 0)
    def _():
        vbuf_ref[0:SUBLANES, :] = jnp.zeros((SUBLANES, D_MODEL), F32)

    x = x_ref[...]
    xb = x.astype(BF16)
    gb = jnp.dot(xb, win_ref[:, 0:D_MODEL], preferred_element_type=F32)
    gc = jnp.dot(xb, win_ref[:, D_MODEL:2 * D_MODEL], preferred_element_type=F32)
    u = jnp.dot(xb, win_ref[:, 2 * D_MODEL:3 * D_MODEL], preferred_element_type=F32)
    v = gc * u
    vbuf_ref[SUBLANES:SUBLANES + tm, :] = v
    cw = convw_ref[...]
    conv = cw[SC_WIDTH - 1:SC_WIDTH, :] * v
    for k in range(SC_WIDTH - 1):
        shift = SC_WIDTH - 1 - k
        conv = conv + cw[k:k + 1, :] * vbuf_ref[SUBLANES - shift:SUBLANES - shift + tm, :]
    vbuf_ref[0:SUBLANES, :] = vbuf_ref[tm:tm + SUBLANES, :]
    y = (gb * conv).astype(BF16)
    hproj = jnp.dot(y, wout_ref[...], preferred_element_type=F32)
    o_ref[...] = _layernorm(DEEPNORM_ALPHA * x + hproj, lng_ref[...], lnb_ref[...])


def _sconv_mixer(x2d, bsz, seq, w_in, conv_w, w_out, ln_g, ln_b):
    t = x2d.shape[0]
    ns = seq // CONV_TM
    assert seq % CONV_TM == 0
    blk = pl.BlockSpec((CONV_TM, D_MODEL), lambda b, s: (b * ns + s, 0))
    return pl.pallas_call(
        _sconv_kernel,
        out_shape=jax.ShapeDtypeStruct((t, D_MODEL), F32),
        grid=(bsz, ns),
        in_specs=[blk, _resident((D_MODEL, 3 * D_MODEL)), _resident((SC_WIDTH, D_MODEL)),
                  _resident((D_MODEL, D_MODEL)), _resident((1, D_MODEL)), _resident((1, D_MODEL))],
        out_specs=blk,
        scratch_shapes=[pltpu.VMEM((CONV_TM + SUBLANES, D_MODEL), F32)],
        compiler_params=_cparams(2), name="sconv_mixer",
    )(x2d, w_in.astype(BF16), conv_w, w_out.astype(BF16), ln_g.reshape(1, -1), ln_b.reshape(1, -1))


def _attn_kernel(q_ref, kp_ref, kc_ref, vp_ref, vc_ref, o_ref, lse_ref, *, dil):
    nq = ATT_BLOCK
    n = pl.program_id(1)
    q = q_ref[...]
    kk = jnp.concatenate([kp_ref[...], kc_ref[...]], axis=0)
    vv = jnp.concatenate([vp_ref[...], vc_ref[...]], axis=0)
    qi = lax.broadcasted_iota(jnp.int32, (nq, 2 * nq), 0)
    kj = lax.broadcasted_iota(jnp.int32, (nq, 2 * nq), 1)
    dist = qi + nq - kj
    first_key = jnp.where(n > 0, 0, nq)
    valid = (dist >= 0) & (dist <= nq) & (kj >= first_key)
    distf = (dist * dil).astype(F32)
    lane = lax.broadcasted_iota(jnp.int32, (1, LANES), 1)
    low_half = lane < ATT_HEAD_DIM
    lane_q = lax.broadcasted_iota(jnp.int32, (nq, LANES), 1)
    lse_all = jnp.zeros((nq, LANES), F32)
    zero_b = jnp.zeros((), BF16)
    for pair in range(ATT_HEADS // 2):
        sl = slice(pair * LANES, (pair + 1) * LANES)
        q_pair, k_pair, v_pair = q[:, sl], kk[:, sl], vv[:, sl]
        outs = []
        for half in range(2):
            h = 2 * pair + half
            slope = 2.0 ** (-8.0 * (h + 1) / ATT_HEADS)
            qm = jnp.where(low_half if half == 0 else jnp.logical_not(low_half), q_pair, zero_b)
            s = lax.dot_general(qm, k_pair, (((1,), (1,)), ((), ())), preferred_element_type=F32)
            s = s * (ATT_HEAD_DIM ** -0.5) - slope * distf
            s = jnp.where(valid, s, -jnp.inf)
            m = jnp.max(s, axis=-1, keepdims=True)
            p = jnp.exp(s - m)
            denom = jnp.sum(p, axis=-1, keepdims=True)
            outs.append(jnp.dot((p / denom).astype(BF16), v_pair, preferred_element_type=F32))
            lse_all = jnp.where(lane_q == h, m + jnp.log(denom), lse_all)
        o_ref[:, sl] = jnp.where(low_half, outs[0], outs[1]).astype(o_ref.dtype)
    lse_ref[...] = lse_all


def _attn_group(qkv, bsz, seq, gi, dil):
    t = qkv.shape[0]
    n_groups = len(ATT_GROUPS)
    span = dil * ATT_BLOCK
    assert seq % span == 0
    nb = seq // span
    n_col = 3 * n_groups
    qkv_v = qkv.reshape(t // dil, dil * n_col * ATT_COLS)
    cq, ck, cv = gi * 3, gi * 3 + 1, gi * 3 + 2
    blk = (ATT_BLOCK, ATT_COLS)
    cur = lambda c: pl.BlockSpec(blk, lambda b, n, r: (b * nb + n, r * n_col + c))
    prev = lambda c: pl.BlockSpec(blk, lambda b, n, r: (b * nb + jnp.maximum(n - 1, 0), r * n_col + c))
    o, lse = pl.pallas_call(
        functools.partial(_attn_kernel, dil=dil),
        out_shape=[jax.ShapeDtypeStruct((t // dil, dil * ATT_COLS), BF16),
                   jax.ShapeDtypeStruct((t // dil, dil * LANES), F32)],
        grid=(bsz, nb, dil),
        in_specs=[cur(cq), prev(ck), cur(ck), prev(cv), cur(cv)],
        out_specs=[pl.BlockSpec(blk, lambda b, n, r: (b * nb + n, r)),
                   pl.BlockSpec((ATT_BLOCK, LANES), lambda b, n, r: (b * nb + n, r))],
        compiler_params=_cparams(3), name=f"dil_attn_g{gi}",
    )(qkv_v, qkv_v, qkv_v, qkv_v, qkv_v)
    return o.reshape(t, ATT_COLS), lse.reshape(t, LANES)


def _attn_merge_kernel(o0_ref, o1_ref, o2_ref, l0_ref, l1_ref, l2_ref, x_ref, e_ref, wout_ref,
                       lng_ref, lnb_ref, out_ref):
    l0, l1, l2 = l0_ref[...], l1_ref[...], l2_ref[...]
    m = jnp.maximum(jnp.maximum(l0, l1), l2)
    e0, e1, e2 = jnp.exp(l0 - m), jnp.exp(l1 - m), jnp.exp(l2 - m)
    den = e0 + e1 + e2
    e_mat = e_ref[...]
    acc = None
    for e, o_ref in ((e0, o0_ref), (e1, o1_ref), (e2, o2_ref)):
        w_x = jnp.dot(_split_hi_lo(e / den), e_mat, preferred_element_type=F32)
        term = w_x * o_ref[...].astype(F32)
        acc = term if acc is None else acc + term
    hproj = jnp.dot(acc.astype(BF16), wout_ref[...], preferred_element_type=F32)
    out_ref[...] = _layernorm(DEEPNORM_ALPHA * x_ref[...] + hproj, lng_ref[...], lnb_ref[...])


def _attn_mixer(x2d, bsz, seq, w_qkv, w_out, ln_g, ln_b):
    t = x2d.shape[0]
    n_groups = len(ATT_GROUPS)
    (qkv,) = _project(x2d, w_qkv.astype(BF16), (n_groups * 3 * ATT_COLS,), name="att_qkv_proj")
    outs, lses = [], []
    for gi, (window, dil) in enumerate(ATT_GROUPS):
        assert window // dil == ATT_BLOCK
        o, l = _attn_group(qkv, bsz, seq, gi, dil)
        outs.append(o)
        lses.append(l)
    row = lambda w: pl.BlockSpec((MERGE_TM, w), lambda i: (i, 0))
    return pl.pallas_call(
        _attn_merge_kernel,
        out_shape=jax.ShapeDtypeStruct((t, D_MODEL), F32),
        grid=(t // MERGE_TM,),
        in_specs=[row(ATT_COLS)] * 3 + [row(LANES)] * 3 + [row(D_MODEL),
                  _resident((2 * LANES, ATT_COLS)), _resident((ATT_COLS, D_MODEL)),
                  _resident((1, D_MODEL)), _resident((1, D_MODEL))],
        out_specs=row(D_MODEL),
        compiler_params=_cparams(1), name="att_merge",
    )(*outs, *lses, x2d, _expansion_matrix(ATT_HEADS, ATT_HEAD_DIM), w_out.astype(BF16),
      ln_g.reshape(1, -1), ln_b.reshape(1, -1))


ROUTE_W1, ROUTE_W2, ROUTE_E1, ROUTE_E2, ROUTE_R1, ROUTE_R2 = range(6)
EXPERT_LANE0 = MOE_GROUPS
N_ZERO_TILES = 2 * MOE_N_EXPERTS


def _router_kernel(x_ref, wr_ref, br_ref, route_ref, cnt_ref):
    tm = ROUTE_TM

    @pl.when(pl.program_id(0) == 0)
    def _():
        cnt_ref[...] = jnp.zeros_like(cnt_ref)

    logits = jnp.dot(x_ref[...], wr_ref[...], precision=HIGHEST,
                     preferred_element_type=F32) + br_ref[...]
    lane_i = lax.broadcasted_iota(jnp.int32, (tm, LANES), 1)
    lane = lane_i.astype(F32)
    big = float(LANES)

    def first_argmax(vals):
        vmax = jnp.max(vals, axis=-1, keepdims=True)
        idx = jnp.min(jnp.where(vals == vmax, lane, big), axis=-1, keepdims=True)
        return vmax, idx

    gl = jnp.where(lane_i < MOE_GROUPS, logits, -jnp.inf)
    gmax, gsel = first_argmax(gl)
    gw = 1.0 / jnp.sum(jnp.exp(gl - gmax), axis=-1, keepdims=True)
    grp_of_lane = jnp.right_shift(lane_i - EXPERT_LANE0, 3).astype(F32)
    in_grp = ((lane_i >= EXPERT_LANE0) & (lane_i < EXPERT_LANE0 + MOE_N_EXPERTS)
              & (grp_of_lane == gsel))
    el = jnp.where(in_grp, logits, -jnp.inf)
    v1, i1 = first_argmax(el)
    v2, i2 = first_argmax(jnp.where(lane == i1, -jnp.inf, el))
    e2 = jnp.exp(v2 - v1)
    w1 = gw / (1.0 + e2)
    w2 = gw * e2 / (1.0 + e2)

    sel1, sel2 = lane == i1, lane == i2
    onehot = jnp.where(sel1 | sel2, 1.0, 0.0)
    r_i = lax.broadcasted_iota(jnp.int32, (tm, tm), 0)
    c_i = lax.broadcasted_iota(jnp.int32, (tm, tm), 1)
    strict = jnp.where(c_i < r_i, 1.0, 0.0).astype(BF16)
    before = jnp.dot(strict, onehot.astype(BF16), preferred_element_type=F32) + cnt_ref[0:1, :]
    rank1 = jnp.sum(jnp.where(sel1, before, 0.0), axis=-1, keepdims=True)
    rank2 = jnp.sum(jnp.where(sel2, before, 0.0), axis=-1, keepdims=True)
    cnt_ref[0:1, :] = cnt_ref[0:1, :] + jnp.sum(onehot, axis=0, keepdims=True)

    out = jnp.zeros((tm, LANES), F32)
    for pos, val in ((ROUTE_W1, w1), (ROUTE_W2, w2), (ROUTE_E1, i1 - EXPERT_LANE0),
                     (ROUTE_E2, i2 - EXPERT_LANE0), (ROUTE_R1, rank1), (ROUTE_R2, rank2)):
        out = jnp.where(lane_i == pos, val, out)
    route_ref[...] = out


def _dispatch_kernel(dest_ref, zrow_ref, zflag_ref, x_ref, xs_hbm, tok_ref, zero_ref, sem, zsem):
    i = pl.program_id(0)
    tm = DISP_TM

    @pl.when(i == 0)
    def _():
        zero_ref[...] = jnp.zeros_like(zero_ref)

        def zcopy(e):
            start = pl.multiple_of(zrow_ref[e] * ROW_TILE, ROW_TILE)
            return pltpu.make_async_copy(zero_ref, xs_hbm.at[pl.ds(start, FFN_TM * ROW_TILE), :], zsem)

        def zstart(e, carry):
            @pl.when(zflag_ref[e] > 0)
            def _():
                zcopy(e).start()
            return carry

        def zwait(e, carry):
            @pl.when(zflag_ref[e] > 0)
            def _():
                zcopy(e).wait()
            return carry

        lax.fori_loop(0, N_ZERO_TILES, zstart, 0)
        lax.fori_loop(0, N_ZERO_TILES, zwait, 0)

    _to_row_tiles(tok_ref, x_ref[...], tm)

    def row_copy(t, k):
        d = pl.multiple_of(dest_ref[(i * tm + t) * MOE_TOP_K + k] * ROW_TILE, ROW_TILE)
        src = tok_ref.at[pl.ds(pl.multiple_of(t * ROW_TILE, ROW_TILE), ROW_TILE), :]
        return pltpu.make_async_copy(src, xs_hbm.at[pl.ds(d, ROW_TILE), :], sem)

    def start(t, carry):
        for k in range(MOE_TOP_K):
            row_copy(t, k).start()
        return carry

    def wait(t, carry):
        for k in range(MOE_TOP_K):
            row_copy(t, k).wait()
        return carry

    lax.fori_loop(0, tm, start, 0)
    lax.fori_loop(0, tm, wait, 0)


def _ffn_kernel(te_ref, nused_ref, xs_ref, wg_ref, wu_ref, wd_ref, ys_ref):
    @pl.when(pl.program_id(0) < nused_ref[0])
    def _():
        x = _from_row_tiles(xs_ref, FFN_TM).astype(BF16)
        g = jnp.dot(x, wg_ref[...].astype(BF16), preferred_element_type=F32)
        u = jnp.dot(x, wu_ref[...].astype(BF16), preferred_element_type=F32)
        hid = (_silu(g) * u).astype(BF16)
        y = jnp.dot(hid, wd_ref[...].astype(BF16), preferred_element_type=F32)
        _to_row_tiles(ys_ref, y, FFN_TM)

    @pl.when(pl.program_id(0) >= nused_ref[0])
    def _():
        ys_ref[...] = jnp.zeros_like(ys_ref)


def _combine_kernel(dest_ref, ys_hbm, route_ref, x_ref, lng_ref, lnb_ref, o_ref,
                    g0_ref, g1_ref, sem):
    i = pl.program_id(0)
    tm = COMB_TM
    bufs = (g0_ref, g1_ref)

    def row_copy(t, k):
        d = pl.multiple_of(dest_ref[(i * tm + t) * MOE_TOP_K + k] * ROW_TILE, ROW_TILE)
        dst = bufs[k].at[pl.ds(pl.multiple_of(t * ROW_TILE, ROW_TILE), ROW_TILE), :]
        return pltpu.make_async_copy(ys_hbm.at[pl.ds(d, ROW_TILE), :], dst, sem)

    def start(t, carry):
        for k in range(MOE_TOP_K):
            row_copy(t, k).start()
        return carry

    def wait(t, carry):
        for k in range(MOE_TOP_K):
            row_copy(t, k).wait()
        return carry

    lax.fori_loop(0, tm, start, 0)
    lax.fori_loop(0, tm, wait, 0)
    route = route_ref[...]
    f = (route[:, ROUTE_W1:ROUTE_W1 + 1] * _from_row_tiles(g0_ref, tm)
         + route[:, ROUTE_W2:ROUTE_W2 + 1] * _from_row_tiles(g1_ref, tm))
    o_ref[...] = _layernorm(DEEPNORM_ALPHA * x_ref[...] + f, lng_ref[...], lnb_ref[...])


def _moe_layer(x2d, layer, wg, bg, we, be, w_gate, w_up, w_down, ln_g, ln_b):
    t = x2d.shape[0]
    n_slots = t * MOE_TOP_K + MOE_N_EXPERTS * FFN_TM
    n_tiles = n_slots // FFN_TM

    pad_cols = LANES - MOE_GROUPS - MOE_N_EXPERTS
    wr = jnp.pad(jnp.concatenate([wg, we], axis=1), ((0, 0), (0, pad_cols)))
    br = jnp.pad(jnp.concatenate([bg, be]), (0, pad_cols)).reshape(1, LANES)
    route, counts = pl.pallas_call(
        _router_kernel,
        out_shape=[jax.ShapeDtypeStruct((t, LANES), F32),
                   jax.ShapeDtypeStruct((SUBLANES, LANES), F32)],
        grid=(t // ROUTE_TM,),
        in_specs=[pl.BlockSpec((ROUTE_TM, D_MODEL), lambda i: (i, 0)),
                  _resident((D_MODEL, LANES)), _resident((1, LANES))],
        out_specs=[pl.BlockSpec((ROUTE_TM, LANES), lambda i: (i, 0)),
                   pl.BlockSpec((SUBLANES, LANES), lambda i: (0, 0))],
        compiler_params=_cparams(1), name="moe_router",
    )(x2d, wr, br)

    cnt = counts[0, EXPERT_LANE0:EXPERT_LANE0 + MOE_N_EXPERTS].astype(jnp.int32)
    padded = ((cnt + FFN_TM - 1) // FFN_TM) * FFN_TM
    ends = jnp.cumsum(padded)
    starts = ends - padded
    experts = route[:, ROUTE_E1:ROUTE_E2 + 1].astype(jnp.int32)
    ranks = route[:, ROUTE_R1:ROUTE_R2 + 1].astype(jnp.int32)
    dest = (jnp.take(starts, experts) + ranks).reshape(-1)
    tile_start = jnp.arange(n_tiles, dtype=jnp.int32) * FFN_TM
    tile_expert = jnp.minimum(jnp.searchsorted(ends, tile_start, side="right"),
                              MOE_N_EXPERTS - 1).astype(jnp.int32)
    n_used = (ends[-1] // FFN_TM).astype(jnp.int32).reshape(1)
    tail_tile = n_tiles - 1 - jnp.arange(MOE_N_EXPERTS, dtype=jnp.int32)
    zrow = jnp.concatenate([jnp.maximum(ends - FFN_TM, 0), tail_tile * FFN_TM]).astype(jnp.int32)
    zflag = jnp.concatenate([padded > 0, tail_tile >= n_used[0]]).astype(jnp.int32)

    xs = pl.pallas_call(
        _dispatch_kernel,
        out_shape=jax.ShapeDtypeStruct((n_slots * ROW_TILE, LANES), F32),
        grid_spec=pltpu.PrefetchScalarGridSpec(
            num_scalar_prefetch=3, grid=(t // DISP_TM,),
            in_specs=[pl.BlockSpec((DISP_TM, D_MODEL), lambda i, *_: (i, 0))],
            out_specs=pl.BlockSpec(memory_space=pl.ANY),
            scratch_shapes=[pltpu.VMEM((DISP_TM * ROW_TILE, LANES), F32),
                            pltpu.VMEM((FFN_TM * ROW_TILE, LANES), F32),
                            pltpu.SemaphoreType.DMA, pltpu.SemaphoreType.DMA]),
        compiler_params=_cparams(1), name="moe_dispatch",
    )(dest, zrow, zflag, x2d)

    w_spec = lambda r, c: pl.BlockSpec((None, None, r, c), lambda i, te, nu: (layer, te[i], 0, 0))
    slot_blk = pl.BlockSpec((FFN_TM * ROW_TILE, LANES), lambda i, te, nu: (i, 0))
    ys = pl.pallas_call(
        _ffn_kernel,
        out_shape=jax.ShapeDtypeStruct((n_slots * ROW_TILE, LANES), F32),
        grid_spec=pltpu.PrefetchScalarGridSpec(
            num_scalar_prefetch=2, grid=(n_tiles,),
            in_specs=[slot_blk, w_spec(D_MODEL, MOE_D_EXPERT), w_spec(D_MODEL, MOE_D_EXPERT),
                      w_spec(MOE_D_EXPERT, D_MODEL)],
            out_specs=slot_blk),
        compiler_params=_cparams(1), name="moe_ffn",
    )(tile_expert, n_used, xs, w_gate, w_up, w_down)

    row = lambda w: pl.BlockSpec((COMB_TM, w), lambda i, *_: (i, 0))
    vec = pl.BlockSpec((1, D_MODEL), lambda i, *_: (0, 0))
    return pl.pallas_call(
        _combine_kernel,
        out_shape=jax.ShapeDtypeStruct((t, D_MODEL), F32),
        grid_spec=pltpu.PrefetchScalarGridSpec(
            num_scalar_prefetch=1, grid=(t // COMB_TM,),
            in_specs=[pl.BlockSpec(memory_space=pl.ANY), row(LANES), row(D_MODEL), vec, vec],
            out_specs=row(D_MODEL),
            scratch_shapes=[pltpu.VMEM((COMB_TM * ROW_TILE, LANES), F32),
                            pltpu.VMEM((COMB_TM * ROW_TILE, LANES), F32),
                            pltpu.SemaphoreType.DMA]),
        compiler_params=_cparams(1), name="moe_combine",
    )(dest, ys, route, x2d, ln_g.reshape(1, -1), ln_b.reshape(1, -1))


def kernel(x, ssd_w_in, ssd_conv_w, ssd_conv_b, ssd_dt_bias, ssd_a_log, ssd_d, ssd_norm_w, ssd_w_out,
           sc_w_in, sc_conv_w, sc_w_out, att_w_qkv, att_w_out,
           moe_wg, moe_bg, moe_we, moe_be, moe_w_gate, moe_w_up, moe_w_down, ln_g, ln_b):
    bsz, seq, d = x.shape
    assert d == D_MODEL
    h = x.reshape(bsz * seq, d)
    for i in range(DEPTH):
        kind, j = i % N_MIXERS, i // N_MIXERS
        if kind == 0:
            h = _ssd_mixer(h, bsz, seq, ssd_w_in[j], ssd_conv_w[j], ssd_conv_b[j], ssd_dt_bias[j],
                           ssd_a_log[j], ssd_d[j], ssd_norm_w[j], ssd_w_out[j], ln_g[i, 0], ln_b[i, 0])
        elif kind == 1:
            h = _sconv_mixer(h, bsz, seq, sc_w_in[j], sc_conv_w[j], sc_w_out[j], ln_g[i, 0], ln_b[i, 0])
        else:
            h = _attn_mixer(h, bsz, seq, att_w_qkv[j], att_w_out[j], ln_g[i, 0], ln_b[i, 0])
        h = _moe_layer(h, i, moe_wg[i], moe_bg[i], moe_we[i], moe_be[i],
                       moe_w_gate, moe_w_up, moe_w_down, ln_g[i, 1], ln_b[i, 1])
    return h.reshape(bsz, seq, d)
```

```python
import functools
import math

import numpy as np
import jax
import jax.numpy as jnp
from jax import lax
from jax.experimental import pallas as pl
from jax.experimental.pallas import tpu as pltpu

D_MODEL = 1024
DEPTH = 4
N_MIXERS = 3
DEEPNORM_ALPHA = (2 * DEPTH) ** 0.25
LN_EPS = 1e-5

SSD_D_INNER = 2 * D_MODEL
SSD_HEAD_DIM = 64
SSD_N_HEADS = SSD_D_INNER // SSD_HEAD_DIM
SSD_N_GROUPS = 4
SSD_D_STATE = 128
SSD_CONV_WIDTH = 4
SSD_CHUNK = 256
SSD_SUBCHUNK = 128
SSD_CONV_DIM = SSD_D_INNER + 2 * SSD_N_GROUPS * SSD_D_STATE
SSD_HEADS_PER_GROUP = SSD_N_HEADS // SSD_N_GROUPS
SSD_GROUP_WIDTH = SSD_D_INNER // SSD_N_GROUPS

SC_WIDTH = 3

ATT_GROUPS = ((128, 1), (512, 4), (2048, 16))
ATT_HEAD_DIM = 64
ATT_HEADS = D_MODEL // ATT_HEAD_DIM
ATT_BLOCK = 128
ATT_COLS = ATT_HEADS * ATT_HEAD_DIM

MOE_GROUPS = 4
MOE_EXPERTS_PER_GROUP = 8
MOE_N_EXPERTS = MOE_GROUPS * MOE_EXPERTS_PER_GROUP
MOE_TOP_K = 2
MOE_D_EXPERT = 256

LANES = 128
SUBLANES = 8
VMEM_LIMIT_BYTES = 56 * 1024 * 1024

PROJ_TM = 512
CONV_TM = 512
MERGE_TM = 512
ROUTE_TM = 512
DISP_TM = 512
COMB_TM = 256
FFN_TM = 256
COL_CHUNK = 512
ROW_TILE = D_MODEL // LANES
assert ROW_TILE == SUBLANES

F32 = jnp.float32
BF16 = jnp.bfloat16
HIGHEST = lax.Precision.HIGHEST


def _cparams(n_axes):
    return pltpu.CompilerParams(dimension_semantics=("arbitrary",) * n_axes,
                                vmem_limit_bytes=VMEM_LIMIT_BYTES)


def _resident(shape):
    nd = len(shape)
    return pl.BlockSpec(shape, lambda *_: (0,) * nd, pipeline_mode=pl.Buffered(1))


def _silu(v):
    return v * (1.0 / (1.0 + jnp.exp(-v)))


def _softplus(v):
    return jnp.maximum(v, 0.0) + jnp.log1p(jnp.exp(-jnp.abs(v)))


def _layernorm(v, g, b):
    mu = jnp.mean(v, axis=-1, keepdims=True)
    d = v - mu
    var = jnp.mean(d * d, axis=-1, keepdims=True)
    return d * lax.rsqrt(var + LN_EPS) * g + b


def _split_hi_lo(v):
    hi = v.astype(BF16)
    lo = (v - hi.astype(F32)).astype(BF16)
    return jnp.concatenate([hi, lo], axis=1)


def _expansion_matrix(n_heads, width):
    e = np.zeros((2 * LANES, n_heads * width), np.float32)
    for h in range(n_heads):
        e[h, h * width:(h + 1) * width] = 1.0
        e[LANES + h, h * width:(h + 1) * width] = 1.0
    return jnp.asarray(e, dtype=BF16)


def _to_row_tiles(dst_ref, v, rows):
    for c in range(ROW_TILE):
        dst_ref[pl.ds(c, rows, stride=ROW_TILE), :] = v[:, c * LANES:(c + 1) * LANES]


def _from_row_tiles(src_ref, rows):
    return jnp.concatenate(
        [src_ref[pl.ds(c, rows, stride=ROW_TILE), :] for c in range(ROW_TILE)], axis=1)


def _dot_3pass(x, w):
    xh = x.astype(BF16)
    xl = (x - xh.astype(F32)).astype(BF16)
    wh = w.astype(BF16)
    wl = (w - wh.astype(F32)).astype(BF16)
    return (jnp.dot(xh, wh, preferred_element_type=F32) + jnp.dot(xl, wh, preferred_element_type=F32)
            + jnp.dot(xh, wl, preferred_element_type=F32))


def _proj_kernel(x_ref, w_ref, wdt_ref, *outs, widths):
    x = x_ref[...]
    xb = x.astype(BF16)
    off = 0
    for j, n in enumerate(widths):
        for c in range(n // COL_CHUNK):
            r = jnp.dot(xb, w_ref[:, off + c * COL_CHUNK: off + (c + 1) * COL_CHUNK],
                        preferred_element_type=F32)
            outs[j][:, c * COL_CHUNK:(c + 1) * COL_CHUNK] = r.astype(outs[j].dtype)
        off += n
    outs[len(widths)][...] = _dot_3pass(x, wdt_ref[...])


def _project(x, w_bf16, widths, w_dt, name):
    t, k = x.shape
    n_total = sum(widths)
    assert w_bf16.shape == (k, n_total) and t % PROJ_TM == 0
    row = lambda n: pl.BlockSpec((PROJ_TM, n), lambda i: (i, 0))
    return pl.pallas_call(
        functools.partial(_proj_kernel, widths=tuple(widths)),
        out_shape=[jax.ShapeDtypeStruct((t, n), BF16) for n in widths]
        + [jax.ShapeDtypeStruct((t, w_dt.shape[1]), F32)],
        grid=(t // PROJ_TM,),
        in_specs=[row(k), _resident((k, n_total)), _resident(w_dt.shape)],
        out_specs=[row(n) for n in widths] + [row(w_dt.shape[1])],
        compiler_params=_cparams(1), name=name)(x, w_bf16, w_dt)


def _ssd_kernel(z_ref, xbc_ref, dt_ref, x_ref, convw_ref, convb_ref, dtb_ref, alog_ref,
                dskip_ref, normw_ref, wout_ref, lng_ref, lnb_ref, e_ref,
                o_ref, state_ref, carry_ref, y_ref):
    L = SSD_CHUNK
    SUB = SSD_SUBCHUNK
    N = SSD_D_STATE
    GW = SSD_GROUP_WIDTH
    c_idx = pl.program_id(1)

    @pl.when(c_idx == 0)
    def _():
        state_ref[...] = jnp.zeros_like(state_ref)
        carry_ref[...] = jnp.zeros_like(carry_ref)

    u_b = xbc_ref[...]
    u = u_b.astype(F32)
    cw = convw_ref[...]
    row = lax.broadcasted_iota(jnp.int32, (L, L), 0)
    col = lax.broadcasted_iota(jnp.int32, (L, L), 1)
    carry = carry_ref[...]
    row8 = lax.broadcasted_iota(jnp.int32, (SUBLANES, SSD_CONV_DIM), 0)
    acc = convb_ref[...] + cw[SSD_CONV_WIDTH - 1:SSD_CONV_WIDTH, :] * u
    head = jnp.zeros((SUBLANES, SSD_CONV_DIM), F32)
    for k in range(1, SSD_CONV_WIDTH):
        wk = cw[SSD_CONV_WIDTH - 1 - k:SSD_CONV_WIDTH - k, :]
        shift_mat = jnp.where(row - col == k, 1.0, 0.0).astype(BF16)
        acc = acc + wk * jnp.dot(shift_mat, u_b, preferred_element_type=F32)
        head = head + wk * jnp.where(row8 < k, pltpu.roll(carry, k, 0), 0.0)
    carry_ref[...] = u[L - SUBLANES:, :]
    acc = jnp.concatenate([acc[:SUBLANES] + head, acc[SUBLANES:]], axis=0)
    act = _silu(acc)
    xs = act[:, :SSD_D_INNER]
    bmat = act[:, SSD_D_INNER:SSD_D_INNER + SSD_N_GROUPS * N]
    cmat = act[:, SSD_D_INNER + SSD_N_GROUPS * N:]

    dt = _softplus(dt_ref[...] + dtb_ref[...])
    a = -jnp.exp(alog_ref[...])
    dta = dt * a
    rs = lax.broadcasted_iota(jnp.int32, (SUB, SUB), 0)
    cs_i = lax.broadcasted_iota(jnp.int32, (SUB, SUB), 1)
    causal = rs >= cs_i
    causal_f = causal.astype(F32)
    e_mat = e_ref[...]
    lane = lax.broadcasted_iota(jnp.int32, (1, LANES), 1)
    low_half = lane < SSD_HEAD_DIM
    dskip = dskip_ref[...]

    for sc in range(L // SUB):
        rows = slice(sc * SUB, (sc + 1) * SUB)
        dt_s = dt[rows]
        cs = jnp.dot(causal_f, dta[rows], precision=HIGHEST, preferred_element_type=F32)
        cs_t = cs.T
        last = cs[SUB - 1:SUB, :]
        xs_s = xs[rows]
        ecs_x = jnp.dot(_split_hi_lo(jnp.exp(cs)), e_mat, preferred_element_type=F32)
        ws_x = jnp.dot(_split_hi_lo(jnp.exp(last - cs) * dt_s), e_mat, preferred_element_type=F32)
        dt_x = jnp.dot(_split_hi_lo(dt_s), e_mat, preferred_element_type=F32)
        elast_x = jnp.dot(_split_hi_lo(jnp.broadcast_to(jnp.exp(last), (SUBLANES, LANES))), e_mat,
                          preferred_element_type=F32)[0:1, :]
        xw_b = (xs_s * ws_x).astype(BF16)
        xdt_b = (xs_s * dt_x).astype(BF16)
        for g in range(SSD_N_GROUPS):
            bg = bmat[rows, g * N:(g + 1) * N]
            cg_b = cmat[rows, g * N:(g + 1) * N].astype(BF16)
            cb = lax.dot_general(cg_b, bg.astype(BF16), (((1,), (1,)), ((), ())),
                                 preferred_element_type=F32)
            s_old = state_ref[g]
            yoff = jnp.dot(cg_b, s_old.astype(BF16), preferred_element_type=F32)
            for pr in range(SSD_HEADS_PER_GROUP // 2):
                pair = g * (SSD_HEADS_PER_GROUP // 2) + pr
                cols = slice(pair * LANES, (pair + 1) * LANES)
                x_pair = xdt_b[:, cols]
                ys = []
                for half in range(2):
                    h = 2 * pair + half
                    seg = cs[:, h:h + 1] - cs_t[h:h + 1, :]
                    mix = cb * jnp.exp(jnp.where(causal, seg, -jnp.inf))
                    ys.append(jnp.dot(mix.astype(BF16), x_pair, preferred_element_type=F32))
                off_cols = slice(pr * LANES, (pr + 1) * LANES)
                y_ref[rows, cols] = (jnp.where(low_half, ys[0], ys[1])
                                     + yoff[:, off_cols] * ecs_x[:, cols] + dskip[:, cols] * xs_s[:, cols])
            upd = jnp.dot(bg.T.astype(BF16), xw_b[:, g * GW:(g + 1) * GW], preferred_element_type=F32)
            state_ref[g] = s_old * elast_x[:, g * GW:(g + 1) * GW] + upd

    y = y_ref[...] * _silu(z_ref[...].astype(F32))
    normw = normw_ref[...]
    yn_parts = []
    for g in range(SSD_N_GROUPS):
        yg = y[:, g * GW:(g + 1) * GW]
        ms = jnp.mean(yg * yg, axis=-1, keepdims=True)
        yn_parts.append((yg * lax.rsqrt(ms + LN_EPS) * normw[:, g * GW:(g + 1) * GW]).astype(BF16))
    yn = jnp.concatenate(yn_parts, axis=1)
    hproj = jnp.dot(yn, wout_ref[...], preferred_element_type=F32)
    o_ref[...] = _layernorm(DEEPNORM_ALPHA * x_ref[...] + hproj, lng_ref[...], lnb_ref[...])


def _ssd_mixer(x2d, bsz, seq, w_in, conv_w, conv_b, dt_bias, a_log, d_skip, norm_w, w_out,
               ln_g, ln_b):
    t = x2d.shape[0]
    nc = seq // SSD_CHUNK
    assert seq % SSD_CHUNK == 0
    n_main = SSD_D_INNER + SSD_CONV_DIM
    w_main = w_in[:, :n_main].astype(BF16)
    w_dt = jnp.pad(w_in[:, n_main:], ((0, 0), (0, LANES - SSD_N_HEADS)))
    z, xbc, dt_raw = _project(x2d, w_main, (SSD_D_INNER, SSD_CONV_DIM), w_dt, name="ssd_in_proj")
    pad = (0, LANES - SSD_N_HEADS)
    dtb = jnp.pad(dt_bias, pad).reshape(1, LANES)
    alog = jnp.pad(a_log, pad).reshape(1, LANES)
    dskip = jnp.repeat(d_skip, SSD_HEAD_DIM).reshape(1, SSD_D_INNER)
    L = SSD_CHUNK
    blk = lambda w: pl.BlockSpec((L, w), lambda b, c: (b * nc + c, 0))
    return pl.pallas_call(
        _ssd_kernel,
        out_shape=jax.ShapeDtypeStruct((t, D_MODEL), F32),
        grid=(bsz, nc),
        in_specs=[blk(SSD_D_INNER), blk(SSD_CONV_DIM), blk(LANES), blk(D_MODEL),
                  _resident((SSD_CONV_WIDTH, SSD_CONV_DIM)), _resident((1, SSD_CONV_DIM)),
                  _resident((1, LANES)), _resident((1, LANES)),
                  _resident((1, SSD_D_INNER)), _resident((1, SSD_D_INNER)),
                  _resident((SSD_D_INNER, D_MODEL)),
                  _resident((1, D_MODEL)), _resident((1, D_MODEL)),
                  _resident((2 * LANES, SSD_D_INNER))],
        out_specs=blk(D_MODEL),
        scratch_shapes=[pltpu.VMEM((SSD_N_GROUPS, SSD_D_STATE, SSD_GROUP_WIDTH), F32),
                        pltpu.VMEM((SUBLANES, SSD_CONV_DIM), F32),
                        pltpu.VMEM((L, SSD_D_INNER), F32)],
        compiler_params=_cparams(2), name="ssd_scan",
    )(z, xbc, dt_raw, x2d, conv_w, conv_b.reshape(1, -1), dtb, alog, dskip,
      norm_w.reshape(1, -1), w_out.astype(BF16), ln_g.reshape(1, -1), ln_b.reshape(1, -1),
      _expansion_matrix(SSD_N_HEADS, SSD_HEAD_DIM))


def _sconv_kernel(x_ref, win_ref, convw_ref, wout_ref, lng_ref, lnb_ref, o_ref, vbuf_ref):
    tm = CONV_TM

    @pl.when(pl.program_id(1) == 0)
    def _():
        vbuf_ref[0:SUBLANES, :] = jnp.zeros((SUBLANES, D_MODEL), F32)

    x = x_ref[...]
    xb = x.astype(BF16)
    gb = jnp.dot(xb, win_ref[:, 0:D_MODEL], preferred_element_type=F32)
    gc = jnp.dot(xb, win_ref[:, D_MODEL:2 * D_MODEL], preferred_element_type=F32)
    u = jnp.dot(xb, win_ref[:, 2 * D_MODEL:3 * D_MODEL], preferred_element_type=F32)
    v = gc * u
    vbuf_ref[SUBLANES:SUBLANES + tm, :] = v
    cw = convw_ref[...]
    conv = cw[SC_WIDTH - 1:SC_WIDTH, :] * v
    for k in range(SC_WIDTH - 1):
        shift = SC_WIDTH - 1 - k
        conv = conv + cw[k:k + 1, :] * vbuf_ref[SUBLANES - shift:SUBLANES - shift + tm, :]
    vbuf_ref[0:SUBLANES, :] = vbuf_ref[tm:tm + SUBLANES, :]
    y = (gb * conv).astype(BF16)
    hproj = jnp.dot(y, wout_ref[...], preferred_element_type=F32)
    o_ref[...] = _layernorm(DEEPNORM_ALPHA * x + hproj, lng_ref[...], lnb_ref[...])


def _sconv_mixer(x2d, bsz, seq, w_in, conv_w, w_out, ln_g, ln_b):
    t = x2d.shape[0]
    ns = seq // CONV_TM
    assert seq % CONV_TM == 0
    blk = pl.BlockSpec((CONV_TM, D_MODEL), lambda b, s: (b * ns + s, 0))
    return pl.pallas_call(
        _sconv_kernel,
        out_shape=jax.ShapeDtypeStruct((t, D_MODEL), F32),
        grid=(bsz, ns),
        in_specs=[blk, _resident((D_MODEL, 3 * D_MODEL)), _resident((SC_WIDTH, D_MODEL)),
                  _resident((D_MODEL, D_MODEL)), _resident((1, D_MODEL)), _resident((1, D_MODEL))],
        out_specs=blk,
        scratch_shapes=[pltpu.VMEM((CONV_TM + SUBLANES, D_MODEL), F32)],
        compiler_params=_cparams(2), name="sconv_mixer",
    )(x2d, w_in.astype(BF16), conv_w, w_out.astype(BF16), ln_g.reshape(1, -1), ln_b.reshape(1, -1))


ATT_TM = 512
QKV_COLS = 3 * ATT_COLS


def _stream_view(arr, dil):
    t, c = arr.shape
    span = ATT_BLOCK * dil
    per_tile = ATT_TM // dil
    tiles_per_span = span // ATT_TM
    view = (t // span, dil, ATT_BLOCK, c)
    spec = pl.BlockSpec((None, dil, per_tile, c),
                        lambda i: (i // tiles_per_span, 0, i % tiles_per_span, 0))
    return view, spec


def _deinterleave_rows(slab_ref, dil):
    n_i = ATT_TM // dil
    rows = [jnp.concatenate([slab_ref[s, pl.ds(r, n_i, stride=dil), :] for s in range(ROW_TILE)],
                            axis=1) for r in range(dil)]
    return jnp.concatenate(rows, axis=0).astype(BF16)


def _interleave_rows(slab_ref, blk_ref, dil):
    n_i, c = blk_ref.shape[1], blk_ref.shape[2]
    for r in range(dil):
        br = blk_ref[r].astype(F32)
        for s in range(c // LANES):
            slab_ref[s, pl.ds(r, n_i, stride=dil), :] = br[:, s * LANES:(s + 1) * LANES]
    return jnp.concatenate([slab_ref[s] for s in range(c // LANES)], axis=1)


def _qkv_proj_kernel(x_ref, w_ref, o0_ref, o1_ref, o2_ref, slab_ref):
    x = x_ref[...]
    for s in range(ROW_TILE):
        slab_ref[s] = x[:, s * LANES:(s + 1) * LANES]
    outs = (o0_ref, o1_ref, o2_ref)
    for gi, (_, dil) in enumerate(ATT_GROUPS):
        xb = x.astype(BF16) if dil == 1 else _deinterleave_rows(slab_ref, dil)
        for c in range(QKV_COLS // COL_CHUNK):
            col = gi * QKV_COLS + c * COL_CHUNK
            r = jnp.dot(xb, w_ref[:, col:col + COL_CHUNK], preferred_element_type=F32)
            dst = slice(c * COL_CHUNK, (c + 1) * COL_CHUNK)
            if dil == 1:
                outs[gi][:, dst] = r.astype(BF16)
            else:
                outs[gi][:, :, dst] = r.reshape(dil, ATT_TM // dil, COL_CHUNK).astype(BF16)


def _attn_kernel(q_ref, kp_ref, kc_ref, vp_ref, vc_ref, o_ref, lse_ref, *, dil):
    nq = ATT_BLOCK
    j = pl.program_id(1)
    q = q_ref[...] * (ATT_HEAD_DIM ** -0.5)
    kk = jnp.concatenate([kp_ref[...], kc_ref[...]], axis=0)
    vv = jnp.concatenate([vp_ref[...], vc_ref[...]], axis=0)
    qi = lax.broadcasted_iota(jnp.int32, (nq, 2 * nq), 0)
    kj = lax.broadcasted_iota(jnp.int32, (nq, 2 * nq), 1)
    dist = qi + nq - kj
    first_key = jnp.where(j >= dil, 0, nq)
    valid = (dist >= 0) & (dist <= nq) & (kj >= first_key)
    pen = jnp.where(valid, (dist * dil).astype(F32), jnp.inf)
    lane = lax.broadcasted_iota(jnp.int32, (1, LANES), 1)
    low_half = lane < ATT_HEAD_DIM
    lane_q = lax.broadcasted_iota(jnp.int32, (nq, LANES), 1)
    lse_all = jnp.zeros((nq, LANES), F32)
    zero_b = jnp.zeros((), BF16)
    for pair in range(ATT_HEADS // 2):
        sl = slice(pair * LANES, (pair + 1) * LANES)
        q_pair, k_pair, v_pair = q[:, sl], kk[:, sl], vv[:, sl]
        outs = []
        for half in range(2):
            h = 2 * pair + half
            slope = 2.0 ** (-8.0 * (h + 1) / ATT_HEADS)
            qm = jnp.where(low_half if half == 0 else jnp.logical_not(low_half), q_pair, zero_b)
            s = lax.dot_general(qm, k_pair, (((1,), (1,)), ((), ())), preferred_element_type=F32)
            s = s - slope * pen
            m = jnp.max(s, axis=-1, keepdims=True)
            p = jnp.exp(s - m)
            denom = jnp.sum(p, axis=-1, keepdims=True)
            pv = jnp.dot(p.astype(BF16), v_pair, preferred_element_type=F32)
            outs.append(pv * (1.0 / denom))
            lse_all = jnp.where(lane_q == h, m + jnp.log(denom), lse_all)
        o_ref[:, sl] = jnp.where(low_half, outs[0], outs[1]).astype(o_ref.dtype)
    lse_ref[...] = lse_all


def _attn_group(qkv_g, bsz, seq, gi, dil):
    t = qkv_g.shape[0]
    assert seq % (dil * ATT_BLOCK) == 0
    nblk = seq // ATT_BLOCK
    blk = (ATT_BLOCK, ATT_COLS)
    cur = lambda c: pl.BlockSpec(blk, lambda b, j: (b * nblk + j, c))
    prev = lambda c: pl.BlockSpec(blk, lambda b, j: (b * nblk + jnp.maximum(j - dil, 0), c))
    return pl.pallas_call(
        functools.partial(_attn_kernel, dil=dil),
        out_shape=[jax.ShapeDtypeStruct((t, ATT_COLS), BF16),
                   jax.ShapeDtypeStruct((t, LANES), F32)],
        grid=(bsz, nblk),
        in_specs=[cur(0), prev(1), cur(1), prev(2), cur(2)],
        out_specs=[pl.BlockSpec(blk, lambda b, j: (b * nblk + j, 0)),
                   pl.BlockSpec((ATT_BLOCK, LANES), lambda b, j: (b * nblk + j, 0))],
        compiler_params=_cparams(2), name=f"dil_attn_g{gi}",
    )(qkv_g, qkv_g, qkv_g, qkv_g, qkv_g)


def _attn_merge_kernel(o0_ref, o1_ref, o2_ref, l0_ref, l1_ref, l2_ref, x_ref, e_ref, wout_ref,
                       lng_ref, lnb_ref, out_ref, oslab_ref, lslab_ref):
    dils = [d for _, d in ATT_GROUPS]
    outs = [o0_ref[...].astype(F32)] + [_interleave_rows(oslab_ref, r, d)
                                        for r, d in ((o1_ref, dils[1]), (o2_ref, dils[2]))]
    l0 = l0_ref[...]
    l1 = _interleave_rows(lslab_ref, l1_ref, dils[1])
    l2 = _interleave_rows(lslab_ref, l2_ref, dils[2])
    m = jnp.maximum(jnp.maximum(l0, l1), l2)
    es = [jnp.exp(l0 - m), jnp.exp(l1 - m), jnp.exp(l2 - m)]
    den = es[0] + es[1] + es[2]
    e_mat = e_ref[...]
    acc = None
    for e, o in zip(es, outs):
        w_x = jnp.dot(_split_hi_lo(e / den), e_mat, preferred_element_type=F32)
        acc = w_x * o if acc is None else acc + w_x * o
    hproj = jnp.dot(acc.astype(BF16), wout_ref[...], preferred_element_type=F32)
    out_ref[...] = _layernorm(DEEPNORM_ALPHA * x_ref[...] + hproj, lng_ref[...], lnb_ref[...])


def _attn_mixer(x2d, bsz, seq, w_qkv, w_out, ln_g, ln_b):
    t = x2d.shape[0]
    n_groups = len(ATT_GROUPS)
    assert n_groups == 3 and ATT_GROUPS[0][1] == 1
    for window, dil in ATT_GROUPS:
        assert window // dil == ATT_BLOCK and seq % (ATT_BLOCK * dil) == 0
        assert ATT_TM % dil == 0 and (ATT_BLOCK * dil) % ATT_TM == 0 or dil == 1
    row = lambda w: pl.BlockSpec((ATT_TM, w), lambda i: (i, 0))

    views, specs = zip(*[_stream_view(jax.ShapeDtypeStruct((t, QKV_COLS), BF16), d)
                         for _, d in ATT_GROUPS[1:]])
    qkvs = pl.pallas_call(
        _qkv_proj_kernel,
        out_shape=[jax.ShapeDtypeStruct((t, QKV_COLS), BF16)]
        + [jax.ShapeDtypeStruct(v, BF16) for v in views],
        grid=(t // ATT_TM,),
        in_specs=[row(D_MODEL), _resident((D_MODEL, n_groups * QKV_COLS))],
        out_specs=[row(QKV_COLS)] + list(specs),
        scratch_shapes=[pltpu.VMEM((ROW_TILE, ATT_TM, LANES), F32)],
        compiler_params=_cparams(1), name="att_qkv_proj",
    )(x2d, w_qkv.astype(BF16))

    outs, lses = [], []
    for gi, (_, dil) in enumerate(ATT_GROUPS):
        o, l = _attn_group(qkvs[gi].reshape(t, QKV_COLS), bsz, seq, gi, dil)
        outs.append(o)
        lses.append(l)

    in_arrays, in_specs = [], []
    for group in (outs, lses):
        for arr, (_, dil) in zip(group, ATT_GROUPS):
            if dil == 1:
                in_arrays.append(arr)
                in_specs.append(row(arr.shape[1]))
            else:
                view, spec = _stream_view(arr, dil)
                in_arrays.append(arr.reshape(view))
                in_specs.append(spec)
    return pl.pallas_call(
        _attn_merge_kernel,
        out_shape=jax.ShapeDtypeStruct((t, D_MODEL), F32),
        grid=(t // ATT_TM,),
        in_specs=in_specs + [row(D_MODEL), _resident((2 * LANES, ATT_COLS)),
                             _resident((ATT_COLS, D_MODEL)),
                             _resident((1, D_MODEL)), _resident((1, D_MODEL))],
        out_specs=row(D_MODEL),
        scratch_shapes=[pltpu.VMEM((ATT_COLS // LANES, ATT_TM, LANES), F32),
                        pltpu.VMEM((1, ATT_TM, LANES), F32)],
        compiler_params=_cparams(1), name="att_merge",
    )(*in_arrays, x2d, _expansion_matrix(ATT_HEADS, ATT_HEAD_DIM), w_out.astype(BF16),
      ln_g.reshape(1, -1), ln_b.reshape(1, -1))


ROUTE_W1, ROUTE_W2, ROUTE_E1, ROUTE_E2, ROUTE_R1, ROUTE_R2 = range(6)
EXPERT_LANE0 = MOE_GROUPS
N_ZERO_TILES = 2 * MOE_N_EXPERTS
RANK_BITS = 20
RANK_MASK = (1 << RANK_BITS) - 1


def _router_kernel(x_ref, wr_ref, br_ref, route_ref, cnt_ref):
    tm = ROUTE_TM

    @pl.when(pl.program_id(0) == 0)
    def _():
        cnt_ref[...] = jnp.zeros_like(cnt_ref)

    logits = _dot_3pass(x_ref[...], wr_ref[...]) + br_ref[...]
    lane_i = lax.broadcasted_iota(jnp.int32, (tm, LANES), 1)
    lane = lane_i.astype(F32)
    big = float(LANES)

    def first_argmax(vals):
        vmax = jnp.max(vals, axis=-1, keepdims=True)
        idx = jnp.min(jnp.where(vals == vmax, lane, big), axis=-1, keepdims=True)
        return vmax, idx

    gl = jnp.where(lane_i < MOE_GROUPS, logits, -jnp.inf)
    gmax, gsel = first_argmax(gl)
    gw = 1.0 / jnp.sum(jnp.exp(gl - gmax), axis=-1, keepdims=True)
    grp_of_lane = jnp.right_shift(lane_i - EXPERT_LANE0, 3).astype(F32)
    in_grp = ((lane_i >= EXPERT_LANE0) & (lane_i < EXPERT_LANE0 + MOE_N_EXPERTS)
              & (grp_of_lane == gsel))
    el = jnp.where(in_grp, logits, -jnp.inf)
    v1, i1 = first_argmax(el)
    v2, i2 = first_argmax(jnp.where(lane == i1, -jnp.inf, el))
    e2 = jnp.exp(v2 - v1)
    w1 = gw / (1.0 + e2)
    w2 = gw * e2 / (1.0 + e2)

    sel1, sel2 = lane == i1, lane == i2
    onehot = jnp.where(sel1 | sel2, 1.0, 0.0)
    r_i = lax.broadcasted_iota(jnp.int32, (tm, tm), 0)
    c_i = lax.broadcasted_iota(jnp.int32, (tm, tm), 1)
    strict = jnp.where(c_i < r_i, 1.0, 0.0).astype(BF16)
    before = jnp.dot(strict, onehot.astype(BF16), preferred_element_type=F32) + cnt_ref[0:1, :]
    rank1 = jnp.sum(jnp.where(sel1, before, 0.0), axis=-1, keepdims=True)
    rank2 = jnp.sum(jnp.where(sel2, before, 0.0), axis=-1, keepdims=True)
    cnt_ref[0:1, :] = cnt_ref[0:1, :] + jnp.sum(onehot, axis=0, keepdims=True)

    out = jnp.zeros((tm, LANES), F32)
    for pos, val in ((ROUTE_W1, w1), (ROUTE_W2, w2), (ROUTE_E1, i1 - EXPERT_LANE0),
                     (ROUTE_E2, i2 - EXPERT_LANE0), (ROUTE_R1, rank1), (ROUTE_R2, rank2)):
        out = jnp.where(lane_i == pos, val, out)
    route_ref[...] = out


def _slot_row(code_ref, starts_ref, idx):
    code = code_ref[idx]
    slot = starts_ref[lax.shift_right_logical(code, RANK_BITS)] + (code & RANK_MASK)
    return pl.multiple_of(slot * ROW_TILE, ROW_TILE)


def _dispatch_kernel(code_ref, starts_ref, zrow_ref, zflag_ref, x_ref, xs_hbm, tok_ref, zero_ref,
                     sem, zsem):
    i = pl.program_id(0)
    tm = DISP_TM

    @pl.when(i == 0)
    def _():
        zero_ref[...] = jnp.zeros_like(zero_ref)

        def zcopy(e):
            start = pl.multiple_of(zrow_ref[e] * ROW_TILE, ROW_TILE)
            return pltpu.make_async_copy(zero_ref, xs_hbm.at[pl.ds(start, FFN_TM * ROW_TILE), :], zsem)

        def zstart(e, carry):
            @pl.when(zflag_ref[e] > 0)
            def _():
                zcopy(e).start()
            return carry

        def zwait(e, carry):
            @pl.when(zflag_ref[e] > 0)
            def _():
                zcopy(e).wait()
            return carry

        lax.fori_loop(0, N_ZERO_TILES, zstart, 0)
        lax.fori_loop(0, N_ZERO_TILES, zwait, 0)

    _to_row_tiles(tok_ref, x_ref[...], tm)

    def row_copy(t, k):
        d = _slot_row(code_ref, starts_ref, (i * tm + t) * MOE_TOP_K + k)
        src = tok_ref.at[pl.ds(pl.multiple_of(t * ROW_TILE, ROW_TILE), ROW_TILE), :]
        return pltpu.make_async_copy(src, xs_hbm.at[pl.ds(d, ROW_TILE), :], sem)

    def start(t, carry):
        for k in range(MOE_TOP_K):
            row_copy(t, k).start(priority=k)
        return carry

    def wait(t, carry):
        for k in range(MOE_TOP_K):
            row_copy(t, k).wait()
        return carry

    lax.fori_loop(0, tm, start, 0)
    lax.fori_loop(0, tm, wait, 0)


def _ffn_kernel(te_ref, nused_ref, xs_ref, wg_ref, wu_ref, wd_ref, ys_ref, wgb_ref, wub_ref, wdb_ref):
    i = pl.program_id(0)

    @pl.when((i == 0) | (te_ref[i] != te_ref[jnp.maximum(i - 1, 0)]))
    def _():
        wgb_ref[...] = wg_ref[...].astype(BF16)
        wub_ref[...] = wu_ref[...].astype(BF16)
        wdb_ref[...] = wd_ref[...].astype(BF16)

    @pl.when(i < nused_ref[0])
    def _():
        x = _from_row_tiles(xs_ref, FFN_TM).astype(BF16)
        g = jnp.dot(x, wgb_ref[...], preferred_element_type=F32)
        u = jnp.dot(x, wub_ref[...], preferred_element_type=F32)
        hid = (_silu(g) * u).astype(BF16)
        y = jnp.dot(hid, wdb_ref[...], preferred_element_type=F32)
        _to_row_tiles(ys_ref, y, FFN_TM)

    @pl.when(i >= nused_ref[0])
    def _():
        ys_ref[...] = jnp.zeros_like(ys_ref)


def _combine_kernel(code_ref, starts_ref, ys_hbm, route_ref, x_ref, lng_ref, lnb_ref, o_ref,
                    g0_ref, g1_ref, sem):
    i = pl.program_id(0)
    tm = COMB_TM
    bufs = (g0_ref, g1_ref)

    def row_copy(t, k):
        d = _slot_row(code_ref, starts_ref, (i * tm + t) * MOE_TOP_K + k)
        dst = bufs[k].at[pl.ds(pl.multiple_of(t * ROW_TILE, ROW_TILE), ROW_TILE), :]
        return pltpu.make_async_copy(ys_hbm.at[pl.ds(d, ROW_TILE), :], dst, sem)

    def start(t, carry):
        for k in range(MOE_TOP_K):
            row_copy(t, k).start(priority=k)
        return carry

    def wait(t, carry):
        for k in range(MOE_TOP_K):
            row_copy(t, k).wait()
        return carry

    lax.fori_loop(0, tm, start, 0)
    lax.fori_loop(0, tm, wait, 0)
    route = route_ref[...]
    f = (route[:, ROUTE_W1:ROUTE_W1 + 1] * _from_row_tiles(g0_ref, tm)
         + route[:, ROUTE_W2:ROUTE_W2 + 1] * _from_row_tiles(g1_ref, tm))
    o_ref[...] = _layernorm(DEEPNORM_ALPHA * x_ref[...] + f, lng_ref[...], lnb_ref[...])


def _moe_layer(x2d, layer, wg, bg, we, be, w_gate, w_up, w_down, ln_g, ln_b):
    t = x2d.shape[0]
    n_slots = t * MOE_TOP_K + MOE_N_EXPERTS * FFN_TM
    n_tiles = n_slots // FFN_TM

    pad_cols = LANES - MOE_GROUPS - MOE_N_EXPERTS
    wr = jnp.pad(jnp.concatenate([wg, we], axis=1), ((0, 0), (0, pad_cols)))
    br = jnp.pad(jnp.concatenate([bg, be]), (0, pad_cols)).reshape(1, LANES)
    route, counts = pl.pallas_call(
        _router_kernel,
        out_shape=[jax.ShapeDtypeStruct((t, LANES), F32),
                   jax.ShapeDtypeStruct((SUBLANES, LANES), F32)],
        grid=(t // ROUTE_TM,),
        in_specs=[pl.BlockSpec((ROUTE_TM, D_MODEL), lambda i: (i, 0)),
                  _resident((D_MODEL, LANES)), _resident((1, LANES))],
        out_specs=[pl.BlockSpec((ROUTE_TM, LANES), lambda i: (i, 0)),
                   pl.BlockSpec((SUBLANES, LANES), lambda i: (0, 0))],
        compiler_params=_cparams(1), name="moe_router",
    )(x2d, wr, br)

    cnt = counts[0, EXPERT_LANE0:EXPERT_LANE0 + MOE_N_EXPERTS].astype(jnp.int32)
    padded = ((cnt + FFN_TM - 1) // FFN_TM) * FFN_TM
    ends = jnp.cumsum(padded)
    starts = ends - padded
    starts = starts.astype(jnp.int32)
    experts = route[:, ROUTE_E1:ROUTE_E2 + 1].astype(jnp.int32)
    ranks = route[:, ROUTE_R1:ROUTE_R2 + 1].astype(jnp.int32)
    code = (experts * (1 << RANK_BITS) + ranks).reshape(-1)
    tile_start = jnp.arange(n_tiles, dtype=jnp.int32) * FFN_TM
    tile_expert = jnp.minimum(jnp.sum(ends[None, :] <= tile_start[:, None], axis=1),
                              MOE_N_EXPERTS - 1).astype(jnp.int32)
    n_used = (ends[-1] // FFN_TM).astype(jnp.int32).reshape(1)
    tail_tile = n_tiles - 1 - jnp.arange(MOE_N_EXPERTS, dtype=jnp.int32)
    zrow = jnp.concatenate([jnp.maximum(ends - FFN_TM, 0), tail_tile * FFN_TM]).astype(jnp.int32)
    zflag = jnp.concatenate([padded > 0, tail_tile >= n_used[0]]).astype(jnp.int32)

    xs = pl.pallas_call(
        _dispatch_kernel,
        out_shape=jax.ShapeDtypeStruct((n_slots * ROW_TILE, LANES), F32),
        grid_spec=pltpu.PrefetchScalarGridSpec(
            num_scalar_prefetch=4, grid=(t // DISP_TM,),
            in_specs=[pl.BlockSpec((DISP_TM, D_MODEL), lambda i, *_: (i, 0))],
            out_specs=pl.BlockSpec(memory_space=pl.ANY),
            scratch_shapes=[pltpu.VMEM((DISP_TM * ROW_TILE, LANES), F32),
                            pltpu.VMEM((FFN_TM * ROW_TILE, LANES), F32),
                            pltpu.SemaphoreType.DMA, pltpu.SemaphoreType.DMA]),
        compiler_params=_cparams(1), name="moe_dispatch",
    )(code, starts, zrow, zflag, x2d)

    w_spec = lambda r, c: pl.BlockSpec((None, None, r, c), lambda i, te, nu: (layer, te[i], 0, 0))
    slot_blk = pl.BlockSpec((FFN_TM * ROW_TILE, LANES), lambda i, te, nu: (i, 0))
    ys = pl.pallas_call(
        _ffn_kernel,
        out_shape=jax.ShapeDtypeStruct((n_slots * ROW_TILE, LANES), F32),
        grid_spec=pltpu.PrefetchScalarGridSpec(
            num_scalar_prefetch=2, grid=(n_tiles,),
            in_specs=[slot_blk, w_spec(D_MODEL, MOE_D_EXPERT), w_spec(D_MODEL, MOE_D_EXPERT),
                      w_spec(MOE_D_EXPERT, D_MODEL)],
            out_specs=slot_blk,
            scratch_shapes=[pltpu.VMEM((D_MODEL, MOE_D_EXPERT), BF16),
                            pltpu.VMEM((D_MODEL, MOE_D_EXPERT), BF16),
                            pltpu.VMEM((MOE_D_EXPERT, D_MODEL), BF16)]),
        compiler_params=_cparams(1), name="moe_ffn",
    )(tile_expert, n_used, xs, w_gate, w_up, w_down)

    row = lambda w: pl.BlockSpec((COMB_TM, w), lambda i, *_: (i, 0))
    vec = pl.BlockSpec((1, D_MODEL), lambda i, *_: (0, 0))
    return pl.pallas_call(
        _combine_kernel,
        out_shape=jax.ShapeDtypeStruct((t, D_MODEL), F32),
        grid_spec=pltpu.PrefetchScalarGridSpec(
            num_scalar_prefetch=2, grid=(t // COMB_TM,),
            in_specs=[pl.BlockSpec(memory_space=pl.ANY), row(LANES), row(D_MODEL), vec, vec],
            out_specs=row(D_MODEL),
            scratch_shapes=[pltpu.VMEM((COMB_TM * ROW_TILE, LANES), F32),
                            pltpu.VMEM((COMB_TM * ROW_TILE, LANES), F32),
                            pltpu.SemaphoreType.DMA]),
        compiler_params=_cparams(1), name="moe_combine",
    )(code, starts, ys, route, x2d, ln_g.reshape(1, -1), ln_b.reshape(1, -1))


def kernel(x, ssd_w_in, ssd_conv_w, ssd_conv_b, ssd_dt_bias, ssd_a_log, ssd_d, ssd_norm_w, ssd_w_out,
           sc_w_in, sc_conv_w, sc_w_out, att_w_qkv, att_w_out,
           moe_wg, moe_bg, moe_we, moe_be, moe_w_gate, moe_w_up, moe_w_down, ln_g, ln_b):
    bsz, seq, d = x.shape
    assert d == D_MODEL
    h = x.reshape(bsz * seq, d)
    for i in range(DEPTH):
        kind, j = i % N_MIXERS, i // N_MIXERS
        if kind == 0:
            h = _ssd_mixer(h, bsz, seq, ssd_w_in[j], ssd_conv_w[j], ssd_conv_b[j], ssd_dt_bias[j],
                           ssd_a_log[j], ssd_d[j], ssd_norm_w[j], ssd_w_out[j], ln_g[i, 0], ln_b[i, 0])
        elif kind == 1:
            h = _sconv_mixer(h, bsz, seq, sc_w_in[j], sc_conv_w[j], sc_w_out[j], ln_g[i, 0], ln_b[i, 0])
        else:
            h = _attn_mixer(h, bsz, seq, att_w_qkv[j], att_w_out[j], ln_g[i, 0], ln_b[i, 0])
        h = _moe_layer(h, i, moe_wg[i], moe_bg[i], moe_we[i], moe_be[i],
                       moe_w_gate, moe_w_up, moe_w_down, ln_g[i, 1], ln_b[i, 1])
    return h.reshape(bsz, seq, d)
```

```python
import functools

import numpy as np
import jax
import jax.numpy as jnp
from jax import lax
from jax.experimental import pallas as pl
from jax.experimental.pallas import tpu as pltpu

D_MODEL = 1024
DEPTH = 4
N_MIXERS = 3
DEEPNORM_ALPHA = (2 * DEPTH) ** 0.25
LN_EPS = 1e-5

SSD_D_INNER = 2 * D_MODEL
SSD_HEAD_DIM = 64
SSD_N_HEADS = SSD_D_INNER // SSD_HEAD_DIM
SSD_N_GROUPS = 4
SSD_D_STATE = 128
SSD_CONV_WIDTH = 4
SSD_CHUNK = 256
SSD_SUBCHUNK = 128
SSD_CONV_DIM = SSD_D_INNER + 2 * SSD_N_GROUPS * SSD_D_STATE
SSD_HEADS_PER_GROUP = SSD_N_HEADS // SSD_N_GROUPS
SSD_GROUP_WIDTH = SSD_D_INNER // SSD_N_GROUPS

SC_WIDTH = 3

ATT_GROUPS = ((128, 1), (512, 4), (2048, 16))
ATT_HEAD_DIM = 64
ATT_HEADS = D_MODEL // ATT_HEAD_DIM
ATT_BLOCK = 128
ATT_COLS = ATT_HEADS * ATT_HEAD_DIM

MOE_GROUPS = 4
MOE_EXPERTS_PER_GROUP = 8
MOE_N_EXPERTS = MOE_GROUPS * MOE_EXPERTS_PER_GROUP
MOE_TOP_K = 2
MOE_D_EXPERT = 256

LANES = 128
SUBLANES = 8
VMEM_LIMIT_BYTES = 56 * 1024 * 1024

PROJ_TM = 512
CONV_TM = 512
MERGE_TM = 512
ROUTE_TM = 512
FFN_TM = 256
COL_CHUNK = 512
ROW_TILE = D_MODEL // LANES

F32 = jnp.float32
BF16 = jnp.bfloat16
HIGHEST = lax.Precision.HIGHEST


def _cparams(n_axes):
    return pltpu.CompilerParams(dimension_semantics=("arbitrary",) * n_axes,
                                vmem_limit_bytes=VMEM_LIMIT_BYTES)


def _resident(shape):
    nd = len(shape)
    return pl.BlockSpec(shape, lambda *_: (0,) * nd, pipeline_mode=pl.Buffered(1))


def _silu(v):
    return v * (1.0 / (1.0 + jnp.exp(-v)))


def _softplus(v):
    return jnp.maximum(v, 0.0) + jnp.log1p(jnp.exp(-jnp.abs(v)))


def _layernorm(v, g, b):
    mu = jnp.mean(v, axis=-1, keepdims=True)
    d = v - mu
    var = jnp.mean(d * d, axis=-1, keepdims=True)
    return d * lax.rsqrt(var + LN_EPS) * g + b


def _split_hi_lo(v):
    hi = v.astype(BF16)
    lo = (v - hi.astype(F32)).astype(BF16)
    return jnp.concatenate([hi, lo], axis=1)


def _expansion_matrix(n_heads, width):
    e = np.zeros((2 * LANES, n_heads * width), np.float32)
    for h in range(n_heads):
        e[h, h * width:(h + 1) * width] = 1.0
        e[LANES + h, h * width:(h + 1) * width] = 1.0
    return jnp.asarray(e, dtype=BF16)


def _dot_3pass(x, w):
    xh = x.astype(BF16)
    xl = (x - xh.astype(F32)).astype(BF16)
    wh = w.astype(BF16)
    wl = (w - wh.astype(F32)).astype(BF16)
    return (jnp.dot(xh, wh, preferred_element_type=F32) + jnp.dot(xl, wh, preferred_element_type=F32)
            + jnp.dot(xh, wl, preferred_element_type=F32))


def _proj_kernel(x_ref, w_ref, wdt_ref, *outs, widths):
    x = x_ref[...]
    xb = x.astype(BF16)
    off = 0
    for j, n in enumerate(widths):
        for c in range(n // COL_CHUNK):
            r = jnp.dot(xb, w_ref[:, off + c * COL_CHUNK: off + (c + 1) * COL_CHUNK],
                        preferred_element_type=F32)
            outs[j][:, c * COL_CHUNK:(c + 1) * COL_CHUNK] = r.astype(outs[j].dtype)
        off += n
    outs[len(widths)][...] = _dot_3pass(x, wdt_ref[...])


def _project(x, w_bf16, widths, w_dt, name):
    t, k = x.shape
    n_total = sum(widths)
    assert w_bf16.shape == (k, n_total) and t % PROJ_TM == 0
    row = lambda n: pl.BlockSpec((PROJ_TM, n), lambda i: (i, 0))
    return pl.pallas_call(
        functools.partial(_proj_kernel, widths=tuple(widths)),
        out_shape=[jax.ShapeDtypeStruct((t, n), BF16) for n in widths]
        + [jax.ShapeDtypeStruct((t, w_dt.shape[1]), F32)],
        grid=(t // PROJ_TM,),
        in_specs=[row(k), _resident((k, n_total)), _resident(w_dt.shape)],
        out_specs=[row(n) for n in widths] + [row(w_dt.shape[1])],
        compiler_params=_cparams(1), name=name)(x, w_bf16, w_dt)


def _ssd_kernel(z_ref, xbc_ref, dt_ref, x_ref, convw_ref, convb_ref, dtb_ref, alog_ref,
                dskip_ref, normw_ref, wout_ref, lng_ref, lnb_ref, e_ref,
                o_ref, state_ref, carry_ref, y_ref):
    L = SSD_CHUNK
    SUB = SSD_SUBCHUNK
    N = SSD_D_STATE
    GW = SSD_GROUP_WIDTH
    c_idx = pl.program_id(1)

    @pl.when(c_idx == 0)
    def _():
        state_ref[...] = jnp.zeros_like(state_ref)
        carry_ref[...] = jnp.zeros_like(carry_ref)

    u_b = xbc_ref[...]
    u = u_b.astype(F32)
    cw = convw_ref[...]
    row = lax.broadcasted_iota(jnp.int32, (L, L), 0)
    col = lax.broadcasted_iota(jnp.int32, (L, L), 1)
    carry = carry_ref[...]
    row8 = lax.broadcasted_iota(jnp.int32, (SUBLANES, SSD_CONV_DIM), 0)
    acc = convb_ref[...] + cw[SSD_CONV_WIDTH - 1:SSD_CONV_WIDTH, :] * u
    head = jnp.zeros((SUBLANES, SSD_CONV_DIM), F32)
    for k in range(1, SSD_CONV_WIDTH):
        wk = cw[SSD_CONV_WIDTH - 1 - k:SSD_CONV_WIDTH - k, :]
        shift_mat = jnp.where(row - col == k, 1.0, 0.0).astype(BF16)
        acc = acc + wk * jnp.dot(shift_mat, u_b, preferred_element_type=F32)
        head = head + wk * jnp.where(row8 < k, pltpu.roll(carry, k, 0), 0.0)
    carry_ref[...] = u[L - SUBLANES:, :]
    acc = jnp.concatenate([acc[:SUBLANES] + head, acc[SUBLANES:]], axis=0)
    act = _silu(acc)
    xs = act[:, :SSD_D_INNER]
    bmat = act[:, SSD_D_INNER:SSD_D_INNER + SSD_N_GROUPS * N]
    cmat = act[:, SSD_D_INNER + SSD_N_GROUPS * N:]

    dt = _softplus(dt_ref[...] + dtb_ref[...])
    a = -jnp.exp(alog_ref[...])
    dta = dt * a
    rs = lax.broadcasted_iota(jnp.int32, (SUB, SUB), 0)
    cs_i = lax.broadcasted_iota(jnp.int32, (SUB, SUB), 1)
    causal = rs >= cs_i
    causal_f = causal.astype(F32)
    e_mat = e_ref[...]
    lane = lax.broadcasted_iota(jnp.int32, (1, LANES), 1)
    low_half = lane < SSD_HEAD_DIM
    dskip = dskip_ref[...]

    for sc in range(L // SUB):
        rows = slice(sc * SUB, (sc + 1) * SUB)
        dt_s = dt[rows]
        cs = jnp.dot(causal_f, dta[rows], precision=HIGHEST, preferred_element_type=F32)
        cs_t = cs.T
        last = cs[SUB - 1:SUB, :]
        xs_s = xs[rows]
        ecs_x = jnp.dot(_split_hi_lo(jnp.exp(cs)), e_mat, preferred_element_type=F32)
        ws_x = jnp.dot(_split_hi_lo(jnp.exp(last - cs) * dt_s), e_mat, preferred_element_type=F32)
        dt_x = jnp.dot(_split_hi_lo(dt_s), e_mat, preferred_element_type=F32)
        elast_x = jnp.dot(_split_hi_lo(jnp.broadcast_to(jnp.exp(last), (SUBLANES, LANES))), e_mat,
                          preferred_element_type=F32)[0:1, :]
        xw_b = (xs_s * ws_x).astype(BF16)
        xdt_b = (xs_s * dt_x).astype(BF16)
        for g in range(SSD_N_GROUPS):
            bg = bmat[rows, g * N:(g + 1) * N]
            cg_b = cmat[rows, g * N:(g + 1) * N].astype(BF16)
            cb = lax.dot_general(cg_b, bg.astype(BF16), (((1,), (1,)), ((), ())),
                                 preferred_element_type=F32)
            s_old = state_ref[g]
            yoff = jnp.dot(cg_b, s_old.astype(BF16), preferred_element_type=F32)
            for pr in range(SSD_HEADS_PER_GROUP // 2):
                pair = g * (SSD_HEADS_PER_GROUP // 2) + pr
                cols = slice(pair * LANES, (pair + 1) * LANES)
                x_pair = xdt_b[:, cols]
                ys = []
                for half in range(2):
                    h = 2 * pair + half
                    seg = cs[:, h:h + 1] - cs_t[h:h + 1, :]
                    mix = cb * jnp.exp(jnp.where(causal, seg, -jnp.inf))
                    ys.append(jnp.dot(mix.astype(BF16), x_pair, preferred_element_type=F32))
                off_cols = slice(pr * LANES, (pr + 1) * LANES)
                y_ref[rows, cols] = (jnp.where(low_half, ys[0], ys[1])
                                     + yoff[:, off_cols] * ecs_x[:, cols] + dskip[:, cols] * xs_s[:, cols])
            upd = jnp.dot(bg.T.astype(BF16), xw_b[:, g * GW:(g + 1) * GW], preferred_element_type=F32)
            state_ref[g] = s_old * elast_x[:, g * GW:(g + 1) * GW] + upd

    y = y_ref[...] * _silu(z_ref[...].astype(F32))
    normw = normw_ref[...]
    yn_parts = []
    for g in range(SSD_N_GROUPS):
        yg = y[:, g * GW:(g + 1) * GW]
        ms = jnp.mean(yg * yg, axis=-1, keepdims=True)
        yn_parts.append((yg * lax.rsqrt(ms + LN_EPS) * normw[:, g * GW:(g + 1) * GW]).astype(BF16))
    yn = jnp.concatenate(yn_parts, axis=1)
    hproj = jnp.dot(yn, wout_ref[...], preferred_element_type=F32)
    o_ref[...] = _layernorm(DEEPNORM_ALPHA * x_ref[...] + hproj, lng_ref[...], lnb_ref[...])


def _ssd_mixer(x2d, bsz, seq, w_in, conv_w, conv_b, dt_bias, a_log, d_skip, norm_w, w_out,
               ln_g, ln_b):
    t = x2d.shape[0]
    nc = seq // SSD_CHUNK
    assert seq % SSD_CHUNK == 0
    n_main = SSD_D_INNER + SSD_CONV_DIM
    w_main = w_in[:, :n_main].astype(BF16)
    w_dt = jnp.pad(w_in[:, n_main:], ((0, 0), (0, LANES - SSD_N_HEADS)))
    z, xbc, dt_raw = _project(x2d, w_main, (SSD_D_INNER, SSD_CONV_DIM), w_dt, name="ssd_in_proj")
    pad = (0, LANES - SSD_N_HEADS)
    dtb = jnp.pad(dt_bias, pad).reshape(1, LANES)
    alog = jnp.pad(a_log, pad).reshape(1, LANES)
    dskip = jnp.repeat(d_skip, SSD_HEAD_DIM).reshape(1, SSD_D_INNER)
    L = SSD_CHUNK
    blk = lambda w: pl.BlockSpec((L, w), lambda b, c: (b * nc + c, 0))
    return pl.pallas_call(
        _ssd_kernel,
        out_shape=jax.ShapeDtypeStruct((t, D_MODEL), F32),
        grid=(bsz, nc),
        in_specs=[blk(SSD_D_INNER), blk(SSD_CONV_DIM), blk(LANES), blk(D_MODEL),
                  _resident((SSD_CONV_WIDTH, SSD_CONV_DIM)), _resident((1, SSD_CONV_DIM)),
                  _resident((1, LANES)), _resident((1, LANES)),
                  _resident((1, SSD_D_INNER)), _resident((1, SSD_D_INNER)),
                  _resident((SSD_D_INNER, D_MODEL)),
                  _resident((1, D_MODEL)), _resident((1, D_MODEL)),
                  _resident((2 * LANES, SSD_D_INNER))],
        out_specs=blk(D_MODEL),
        scratch_shapes=[pltpu.VMEM((SSD_N_GROUPS, SSD_D_STATE, SSD_GROUP_WIDTH), F32),
                        pltpu.VMEM((SUBLANES, SSD_CONV_DIM), F32),
                        pltpu.VMEM((L, SSD_D_INNER), F32)],
        compiler_params=_cparams(2), name="ssd_scan",
    )(z, xbc, dt_raw, x2d, conv_w, conv_b.reshape(1, -1), dtb, alog, dskip,
      norm_w.reshape(1, -1), w_out.astype(BF16), ln_g.reshape(1, -1), ln_b.reshape(1, -1),
      _expansion_matrix(SSD_N_HEADS, SSD_HEAD_DIM))


def _sconv_kernel(x_ref, win_ref, convw_ref, wout_ref, lng_ref, lnb_ref, o_ref, vbuf_ref):
    tm = CONV_TM

    @pl.when(pl.program_id(1) == 0)
    def _():
        vbuf_ref[0:SUBLANES, :] = jnp.zeros((SUBLANES, D_MODEL), F32)

    x = x_ref[...]
    xb = x.astype(BF16)
    gb = jnp.dot(xb, win_ref[:, 0:D_MODEL], preferred_element_type=F32)
    gc = jnp.dot(xb, win_ref[:, D_MODEL:2 * D_MODEL], preferred_element_type=F32)
    u = jnp.dot(xb, win_ref[:, 2 * D_MODEL:3 * D_MODEL], preferred_element_type=F32)
    v = gc * u
    vbuf_ref[SUBLANES:SUBLANES + tm, :] = v
    cw = convw_ref[...]
    conv = cw[SC_WIDTH - 1:SC_WIDTH, :] * v
    for k in range(SC_WIDTH - 1):
        shift = SC_WIDTH - 1 - k
        conv = conv + cw[k:k + 1, :] * vbuf_ref[SUBLANES - shift:SUBLANES - shift + tm, :]
    vbuf_ref[0:SUBLANES, :] = vbuf_ref[tm:tm + SUBLANES, :]
    y = (gb * conv).astype(BF16)
    hproj = jnp.dot(y, wout_ref[...], preferred_element_type=F32)
    o_ref[...] = _layernorm(DEEPNORM_ALPHA * x + hproj, lng_ref[...], lnb_ref[...])


def _sconv_mixer(x2d, bsz, seq, w_in, conv_w, w_out, ln_g, ln_b):
    t = x2d.shape[0]
    ns = seq // CONV_TM
    assert seq % CONV_TM == 0
    blk = pl.BlockSpec((CONV_TM, D_MODEL), lambda b, s: (b * ns + s, 0))
    return pl.pallas_call(
        _sconv_kernel,
        out_shape=jax.ShapeDtypeStruct((t, D_MODEL), F32),
        grid=(bsz, ns),
        in_specs=[blk, _resident((D_MODEL, 3 * D_MODEL)), _resident((SC_WIDTH, D_MODEL)),
                  _resident((D_MODEL, D_MODEL)), _resident((1, D_MODEL)), _resident((1, D_MODEL))],
        out_specs=blk,
        scratch_shapes=[pltpu.VMEM((CONV_TM + SUBLANES, D_MODEL), F32)],
        compiler_params=_cparams(2), name="sconv_mixer",
    )(x2d, w_in.astype(BF16), conv_w, w_out.astype(BF16), ln_g.reshape(1, -1), ln_b.reshape(1, -1))


ATT_TM = 512
QKV_COLS = 3 * ATT_COLS


def _stream_view(arr, dil):
    t, c = arr.shape
    span = ATT_BLOCK * dil
    per_tile = ATT_TM // dil
    tiles_per_span = span // ATT_TM
    view = (t // span, dil, ATT_BLOCK, c)
    spec = pl.BlockSpec((None, dil, per_tile, c),
                        lambda i: (i // tiles_per_span, 0, i % tiles_per_span, 0))
    return view, spec


def _deinterleave_rows(slab_ref, dil):
    n_i = ATT_TM // dil
    rows = [jnp.concatenate([slab_ref[s, pl.ds(r, n_i, stride=dil), :] for s in range(ROW_TILE)],
                            axis=1) for r in range(dil)]
    return jnp.concatenate(rows, axis=0).astype(BF16)


def _interleave_rows(slab_ref, blk_ref, dil):
    n_i, c = blk_ref.shape[1], blk_ref.shape[2]
    for r in range(dil):
        br = blk_ref[r].astype(F32)
        for s in range(c // LANES):
            slab_ref[s, pl.ds(r, n_i, stride=dil), :] = br[:, s * LANES:(s + 1) * LANES]
    return jnp.concatenate([slab_ref[s] for s in range(c // LANES)], axis=1)


def _qkv_proj_kernel(x_ref, w_ref, o0_ref, o1_ref, o2_ref, slab_ref):
    x = x_ref[...]
    for s in range(ROW_TILE):
        slab_ref[s] = x[:, s * LANES:(s + 1) * LANES]
    outs = (o0_ref, o1_ref, o2_ref)
    for gi, (_, dil) in enumerate(ATT_GROUPS):
        xb = x.astype(BF16) if dil == 1 else _deinterleave_rows(slab_ref, dil)
        for c in range(QKV_COLS // COL_CHUNK):
            col = gi * QKV_COLS + c * COL_CHUNK
            r = jnp.dot(xb, w_ref[:, col:col + COL_CHUNK], preferred_element_type=F32)
            dst = slice(c * COL_CHUNK, (c + 1) * COL_CHUNK)
            if dil == 1:
                outs[gi][:, dst] = r.astype(BF16)
            else:
                outs[gi][:, :, dst] = r.reshape(dil, ATT_TM // dil, COL_CHUNK).astype(BF16)


def _attn_kernel(q_ref, kp_ref, kc_ref, vp_ref, vc_ref, o_ref, lse_ref, *, dil):
    nq = ATT_BLOCK
    j = pl.program_id(1)
    q = q_ref[...] * (ATT_HEAD_DIM ** -0.5)
    kk = jnp.concatenate([kp_ref[...], kc_ref[...]], axis=0)
    vv = jnp.concatenate([vp_ref[...], vc_ref[...]], axis=0)
    qi = lax.broadcasted_iota(jnp.int32, (nq, 2 * nq), 0)
    kj = lax.broadcasted_iota(jnp.int32, (nq, 2 * nq), 1)
    dist = qi + nq - kj
    first_key = jnp.where(j >= dil, 0, nq)
    valid = (dist >= 0) & (dist <= nq) & (kj >= first_key)
    pen = jnp.where(valid, (dist * dil).astype(F32), jnp.inf)
    lane = lax.broadcasted_iota(jnp.int32, (1, LANES), 1)
    low_half = lane < ATT_HEAD_DIM
    lane_q = lax.broadcasted_iota(jnp.int32, (nq, LANES), 1)
    lse_all = jnp.zeros((nq, LANES), F32)
    zero_b = jnp.zeros((), BF16)
    for pair in range(ATT_HEADS // 2):
        sl = slice(pair * LANES, (pair + 1) * LANES)
        q_pair, k_pair, v_pair = q[:, sl], kk[:, sl], vv[:, sl]
        outs = []
        for half in range(2):
            h = 2 * pair + half
            slope = 2.0 ** (-8.0 * (h + 1) / ATT_HEADS)
            qm = jnp.where(low_half if half == 0 else jnp.logical_not(low_half), q_pair, zero_b)
            s = lax.dot_general(qm, k_pair, (((1,), (1,)), ((), ())), preferred_element_type=F32)
            s = s - slope * pen
            m = jnp.max(s, axis=-1, keepdims=True)
            p = jnp.exp(s - m)
            denom = jnp.sum(p, axis=-1, keepdims=True)
            pv = jnp.dot(p.astype(BF16), v_pair, preferred_element_type=F32)
            outs.append(pv * (1.0 / denom))
            lse_all = jnp.where(lane_q == h, m + jnp.log(denom), lse_all)
        o_ref[:, sl] = jnp.where(low_half, outs[0], outs[1]).astype(o_ref.dtype)
    lse_ref[...] = lse_all


def _attn_group(qkv_g, bsz, seq, gi, dil):
    t = qkv_g.shape[0]
    assert seq % (dil * ATT_BLOCK) == 0
    nblk = seq // ATT_BLOCK
    blk = (ATT_BLOCK, ATT_COLS)
    cur = lambda c: pl.BlockSpec(blk, lambda b, j: (b * nblk + j, c))
    prev = lambda c: pl.BlockSpec(blk, lambda b, j: (b * nblk + jnp.maximum(j - dil, 0), c))
    return pl.pallas_call(
        functools.partial(_attn_kernel, dil=dil),
        out_shape=[jax.ShapeDtypeStruct((t, ATT_COLS), BF16),
                   jax.ShapeDtypeStruct((t, LANES), F32)],
        grid=(bsz, nblk),
        in_specs=[cur(0), prev(1), cur(1), prev(2), cur(2)],
        out_specs=[pl.BlockSpec(blk, lambda b, j: (b * nblk + j, 0)),
                   pl.BlockSpec((ATT_BLOCK, LANES), lambda b, j: (b * nblk + j, 0))],
        compiler_params=_cparams(2), name=f"dil_attn_g{gi}",
    )(qkv_g, qkv_g, qkv_g, qkv_g, qkv_g)


def _attn_merge_kernel(o0_ref, o1_ref, o2_ref, l0_ref, l1_ref, l2_ref, x_ref, e_ref, wout_ref,
                       lng_ref, lnb_ref, out_ref, oslab_ref, lslab_ref):
    dils = [d for _, d in ATT_GROUPS]
    outs = [o0_ref[...].astype(F32)] + [_interleave_rows(oslab_ref, r, d)
                                        for r, d in ((o1_ref, dils[1]), (o2_ref, dils[2]))]
    l0 = l0_ref[...]
    l1 = _interleave_rows(lslab_ref, l1_ref, dils[1])
    l2 = _interleave_rows(lslab_ref, l2_ref, dils[2])
    m = jnp.maximum(jnp.maximum(l0, l1), l2)
    es = [jnp.exp(l0 - m), jnp.exp(l1 - m), jnp.exp(l2 - m)]
    den = es[0] + es[1] + es[2]
    e_mat = e_ref[...]
    acc = None
    for e, o in zip(es, outs):
        w_x = jnp.dot(_split_hi_lo(e / den), e_mat, preferred_element_type=F32)
        acc = w_x * o if acc is None else acc + w_x * o
    hproj = jnp.dot(acc.astype(BF16), wout_ref[...], preferred_element_type=F32)
    out_ref[...] = _layernorm(DEEPNORM_ALPHA * x_ref[...] + hproj, lng_ref[...], lnb_ref[...])


def _attn_mixer(x2d, bsz, seq, w_qkv, w_out, ln_g, ln_b):
    t = x2d.shape[0]
    n_groups = len(ATT_GROUPS)
    assert n_groups == 3 and ATT_GROUPS[0][1] == 1
    for window, dil in ATT_GROUPS:
        assert window // dil == ATT_BLOCK and seq % (ATT_BLOCK * dil) == 0
        assert ATT_TM % dil == 0 and (ATT_BLOCK * dil) % ATT_TM == 0 or dil == 1
    row = lambda w: pl.BlockSpec((ATT_TM, w), lambda i: (i, 0))

    views, specs = zip(*[_stream_view(jax.ShapeDtypeStruct((t, QKV_COLS), BF16), d)
                         for _, d in ATT_GROUPS[1:]])
    qkvs = pl.pallas_call(
        _qkv_proj_kernel,
        out_shape=[jax.ShapeDtypeStruct((t, QKV_COLS), BF16)]
        + [jax.ShapeDtypeStruct(v, BF16) for v in views],
        grid=(t // ATT_TM,),
        in_specs=[row(D_MODEL), _resident((D_MODEL, n_groups * QKV_COLS))],
        out_specs=[row(QKV_COLS)] + list(specs),
        scratch_shapes=[pltpu.VMEM((ROW_TILE, ATT_TM, LANES), F32)],
        compiler_params=_cparams(1), name="att_qkv_proj",
    )(x2d, w_qkv.astype(BF16))

    outs, lses = [], []
    for gi, (_, dil) in enumerate(ATT_GROUPS):
        o, l = _attn_group(qkvs[gi].reshape(t, QKV_COLS), bsz, seq, gi, dil)
        outs.append(o)
        lses.append(l)

    in_arrays, in_specs = [], []
    for group in (outs, lses):
        for arr, (_, dil) in zip(group, ATT_GROUPS):
            if dil == 1:
                in_arrays.append(arr)
                in_specs.append(row(arr.shape[1]))
            else:
                view, spec = _stream_view(arr, dil)
                in_arrays.append(arr.reshape(view))
                in_specs.append(spec)
    return pl.pallas_call(
        _attn_merge_kernel,
        out_shape=jax.ShapeDtypeStruct((t, D_MODEL), F32),
        grid=(t // ATT_TM,),
        in_specs=in_specs + [row(D_MODEL), _resident((2 * LANES, ATT_COLS)),
                             _resident((ATT_COLS, D_MODEL)),
                             _resident((1, D_MODEL)), _resident((1, D_MODEL))],
        out_specs=row(D_MODEL),
        scratch_shapes=[pltpu.VMEM((ATT_COLS // LANES, ATT_TM, LANES), F32),
                        pltpu.VMEM((1, ATT_TM, LANES), F32)],
        compiler_params=_cparams(1), name="att_merge",
    )(*in_arrays, x2d, _expansion_matrix(ATT_HEADS, ATT_HEAD_DIM), w_out.astype(BF16),
      ln_g.reshape(1, -1), ln_b.reshape(1, -1))


ROUTE_W1, ROUTE_W2, ROUTE_E1, ROUTE_E2, ROUTE_S1, ROUTE_S2 = range(6)
EXPERT_LANE0 = MOE_GROUPS
MOE_TB = 512
RUN_ALIGN = 16
LOCAL_SLOTS = 1536
assert LOCAL_SLOTS >= MOE_TOP_K * MOE_TB + MOE_N_EXPERTS * (RUN_ALIGN - 1)
assert ROUTE_TM == MOE_TB


def _n_slot_tiles(t):
    n_blocks = t // MOE_TB
    worst = t * MOE_TOP_K + n_blocks * MOE_N_EXPERTS * (RUN_ALIGN - 1) + MOE_N_EXPERTS * (FFN_TM - 1)
    return -(-worst // FFN_TM), (t * MOE_TOP_K) // FFN_TM


def _router_kernel(x_ref, wr_ref, br_ref, route_ref, cnt_ref):
    tm = ROUTE_TM
    logits = _dot_3pass(x_ref[...], wr_ref[...]) + br_ref[...]
    lane_i = lax.broadcasted_iota(jnp.int32, (tm, LANES), 1)
    lane = lane_i.astype(F32)
    big = float(LANES)

    def first_argmax(vals):
        vmax = jnp.max(vals, axis=-1, keepdims=True)
        idx = jnp.min(jnp.where(vals == vmax, lane, big), axis=-1, keepdims=True)
        return vmax, idx

    gl = jnp.where(lane_i < MOE_GROUPS, logits, -jnp.inf)
    gmax, gsel = first_argmax(gl)
    gw = 1.0 / jnp.sum(jnp.exp(gl - gmax), axis=-1, keepdims=True)
    grp_of_lane = jnp.right_shift(lane_i - EXPERT_LANE0, 3).astype(F32)
    in_grp = ((lane_i >= EXPERT_LANE0) & (lane_i < EXPERT_LANE0 + MOE_N_EXPERTS)
              & (grp_of_lane == gsel))
    el = jnp.where(in_grp, logits, -jnp.inf)
    v1, i1 = first_argmax(el)
    v2, i2 = first_argmax(jnp.where(lane == i1, -jnp.inf, el))
    e2 = jnp.exp(v2 - v1)
    w1 = gw / (1.0 + e2)
    w2 = gw * e2 / (1.0 + e2)

    sel1, sel2 = lane == i1, lane == i2
    onehot = jnp.where(sel1 | sel2, 1.0, 0.0)
    r_i = lax.broadcasted_iota(jnp.int32, (tm, tm), 0)
    c_i = lax.broadcasted_iota(jnp.int32, (tm, tm), 1)
    strict = jnp.where(c_i < r_i, 1.0, 0.0).astype(BF16)
    before = jnp.dot(strict, onehot.astype(BF16), preferred_element_type=F32)
    cnt = jnp.sum(onehot, axis=0, keepdims=True)
    run_tiles = jnp.floor((cnt + (RUN_ALIGN - 1)) * (1.0 / RUN_ALIGN))
    e_r = lax.broadcasted_iota(jnp.int32, (LANES, LANES), 0)
    e_c = lax.broadcasted_iota(jnp.int32, (LANES, LANES), 1)
    earlier = jnp.where(e_r < e_c, 1.0, 0.0).astype(BF16)
    run_start = RUN_ALIGN * jnp.dot(jnp.broadcast_to(run_tiles, (SUBLANES, LANES)).astype(BF16),
                                    earlier, preferred_element_type=F32)[0:1, :]
    slot = before + run_start
    slot1 = jnp.sum(jnp.where(sel1, slot, 0.0), axis=-1, keepdims=True)
    slot2 = jnp.sum(jnp.where(sel2, slot, 0.0), axis=-1, keepdims=True)
    cnt_ref[...] = jnp.broadcast_to(cnt, (SUBLANES, LANES))

    out = jnp.zeros((tm, LANES), F32)
    for pos, val in ((ROUTE_W1, w1), (ROUTE_W2, w2), (ROUTE_E1, i1 - EXPERT_LANE0),
                     (ROUTE_E2, i2 - EXPERT_LANE0), (ROUTE_S1, slot1), (ROUTE_S2, slot2)):
        out = jnp.where(lane_i == pos, val, out)
    route_ref[...] = out


def _run_copies(tabs, blk, make_copy):
    lstart_ref, gstart_ref, npiece_ref, ntot_ref = tabs

    def per_expert(e, carry):
        idx = blk * MOE_N_EXPERTS + e
        l0, g0 = lstart_ref[idx], gstart_ref[idx]

        def per_piece(j, c):
            make_copy(pl.multiple_of(l0 + j * RUN_ALIGN, RUN_ALIGN),
                      pl.multiple_of(g0 + j * RUN_ALIGN, RUN_ALIGN)).start()
            return c

        return lax.fori_loop(0, npiece_ref[idx], per_piece, carry)

    lax.fori_loop(0, MOE_N_EXPERTS, per_expert, 0)

    def wait_one(j, c):
        make_copy(0, 0).wait()
        return c

    lax.fori_loop(0, ntot_ref[blk], wait_one, 0)


def _dispatch_kernel(lstart_ref, gstart_ref, npiece_ref, ntot_ref, zrow_ref, zflag_ref,
                     x_ref, route_ref, xs_hbm, loc_ref, zero_ref, sem, zsem):
    i = pl.program_id(0)
    n_zero = zrow_ref.shape[0]

    @pl.when(i == 0)
    def _():
        zero_ref[...] = jnp.zeros_like(zero_ref)

        def zcopy(e):
            start = pl.multiple_of(zrow_ref[e], FFN_TM)
            return pltpu.make_async_copy(zero_ref, xs_hbm.at[pl.ds(start, FFN_TM), :], zsem)

        def zstart(e, carry):
            @pl.when(zflag_ref[e] > 0)
            def _():
                zcopy(e).start()
            return carry

        def zwait(e, carry):
            @pl.when(zflag_ref[e] > 0)
            def _():
                zcopy(e).wait()
            return carry

        lax.fori_loop(0, n_zero, zstart, 0)
        lax.fori_loop(0, n_zero, zwait, 0)

    slots_t = route_ref[...].T
    s1 = slots_t[ROUTE_S1:ROUTE_S1 + 1, :]
    s2 = slots_t[ROUTE_S2:ROUTE_S2 + 1, :]
    xb = x_ref[...].astype(BF16)
    piece = MOE_TB
    for p in range(LOCAL_SLOTS // piece):
        srow = (lax.broadcasted_iota(jnp.int32, (piece, MOE_TB), 0) + p * piece).astype(F32)
        perm = jnp.where((srow == s1) | (srow == s2), 1.0, 0.0).astype(BF16)
        loc_ref[p * piece:(p + 1) * piece, :] = jnp.dot(
            perm, xb, preferred_element_type=F32).astype(BF16)

    def make_copy(l, g):
        return pltpu.make_async_copy(loc_ref.at[pl.ds(l, RUN_ALIGN), :],
                                     xs_hbm.at[pl.ds(g, RUN_ALIGN), :], sem)

    _run_copies((lstart_ref, gstart_ref, npiece_ref, ntot_ref), i, make_copy)


def _ffn_kernel(te_ref, nused_ref, xs_ref, wg_ref, wu_ref, wd_ref, ys_ref, wgb_ref, wub_ref, wdb_ref):
    i = pl.program_id(0)

    @pl.when((i == 0) | (te_ref[i] != te_ref[jnp.maximum(i - 1, 0)]))
    def _():
        wgb_ref[...] = wg_ref[...].astype(BF16)
        wub_ref[...] = wu_ref[...].astype(BF16)
        wdb_ref[...] = wd_ref[...].astype(BF16)

    @pl.when(i < nused_ref[0])
    def _():
        x = xs_ref[...]
        g = jnp.dot(x, wgb_ref[...], preferred_element_type=F32)
        u = jnp.dot(x, wub_ref[...], preferred_element_type=F32)
        hid = (_silu(g) * u).astype(BF16)
        ys_ref[...] = jnp.dot(hid, wdb_ref[...], preferred_element_type=F32).astype(ys_ref.dtype)

    @pl.when(i >= nused_ref[0])
    def _():
        ys_ref[...] = jnp.zeros_like(ys_ref)


def _combine_kernel(lstart_ref, gstart_ref, npiece_ref, ntot_ref, ys_hbm, route_ref, x_ref,
                    lng_ref, lnb_ref, o_ref, loc_ref, sem):
    i = pl.program_id(0)

    @pl.when(i == 0)
    def _():
        loc_ref[...] = jnp.zeros_like(loc_ref)

    def make_copy(l, g):
        return pltpu.make_async_copy(ys_hbm.at[pl.ds(g, RUN_ALIGN), :],
                                     loc_ref.at[pl.ds(l, RUN_ALIGN), :], sem)

    _run_copies((lstart_ref, gstart_ref, npiece_ref, ntot_ref), i, make_copy)

    route = route_ref[...]
    w1, w2 = route[:, ROUTE_W1:ROUTE_W1 + 1], route[:, ROUTE_W2:ROUTE_W2 + 1]
    s1, s2 = route[:, ROUTE_S1:ROUTE_S1 + 1], route[:, ROUTE_S2:ROUTE_S2 + 1]
    piece = MOE_TB
    f = None
    for p in range(LOCAL_SLOTS // piece):
        scol = (lax.broadcasted_iota(jnp.int32, (MOE_TB, piece), 1) + p * piece).astype(F32)
        comb = (jnp.where(scol == s1, w1, 0.0) + jnp.where(scol == s2, w2, 0.0)).astype(BF16)
        part = jnp.dot(comb, loc_ref[p * piece:(p + 1) * piece, :], preferred_element_type=F32)
        f = part if f is None else f + part
    o_ref[...] = _layernorm(DEEPNORM_ALPHA * x_ref[...] + f, lng_ref[...], lnb_ref[...])


def _moe_layer(x2d, layer, wg, bg, we, be, w_gate, w_up, w_down, ln_g, ln_b):
    t = x2d.shape[0]
    assert t % MOE_TB == 0
    n_blocks = t // MOE_TB
    n_tiles, min_tiles = _n_slot_tiles(t)
    n_slots = n_tiles * FFN_TM

    pad_cols = LANES - MOE_GROUPS - MOE_N_EXPERTS
    wr = jnp.pad(jnp.concatenate([wg, we], axis=1), ((0, 0), (0, pad_cols)))
    br = jnp.pad(jnp.concatenate([bg, be]), (0, pad_cols)).reshape(1, LANES)
    route, counts = pl.pallas_call(
        _router_kernel,
        out_shape=[jax.ShapeDtypeStruct((t, LANES), F32),
                   jax.ShapeDtypeStruct((n_blocks * SUBLANES, LANES), F32)],
        grid=(n_blocks,),
        in_specs=[pl.BlockSpec((ROUTE_TM, D_MODEL), lambda i: (i, 0)),
                  _resident((D_MODEL, LANES)), _resident((1, LANES))],
        out_specs=[pl.BlockSpec((ROUTE_TM, LANES), lambda i: (i, 0)),
                   pl.BlockSpec((SUBLANES, LANES), lambda i: (i, 0))],
        compiler_params=_cparams(1), name="moe_router",
    )(x2d, wr, br)

    cnt = counts.reshape(n_blocks, SUBLANES, LANES)[:, 0, EXPERT_LANE0:EXPERT_LANE0 + MOE_N_EXPERTS]
    cnt = cnt.astype(jnp.int32)
    run = ((cnt + RUN_ALIGN - 1) // RUN_ALIGN) * RUN_ALIGN
    lstart = jnp.cumsum(run, axis=1) - run
    seg_len = jnp.sum(run, axis=0)
    seg_pad = ((seg_len + FFN_TM - 1) // FFN_TM) * FFN_TM
    ends = jnp.cumsum(seg_pad)
    gstart = (ends - seg_pad)[None, :] + jnp.cumsum(run, axis=0) - run
    npiece = run // RUN_ALIGN
    tabs = [a.astype(jnp.int32).reshape(-1) for a in (lstart, gstart, npiece)]
    tabs.append(jnp.sum(npiece, axis=1).astype(jnp.int32))
    tile_start = jnp.arange(n_tiles, dtype=jnp.int32) * FFN_TM
    tile_expert = jnp.minimum(jnp.sum(ends[None, :] <= tile_start[:, None], axis=1),
                              MOE_N_EXPERTS - 1).astype(jnp.int32)
    n_used = (ends[-1] // FFN_TM).astype(jnp.int32).reshape(1)
    tail_tile = jnp.arange(min_tiles, n_tiles, dtype=jnp.int32)
    zrow = jnp.concatenate([jnp.maximum(ends - FFN_TM, 0), tail_tile * FFN_TM]).astype(jnp.int32)
    zflag = jnp.concatenate([seg_pad > 0, tail_tile >= n_used[0]]).astype(jnp.int32)

    blk_row = lambda w: pl.BlockSpec((MOE_TB, w), lambda i, *_: (i, 0))
    xs = pl.pallas_call(
        _dispatch_kernel,
        out_shape=jax.ShapeDtypeStruct((n_slots, D_MODEL), BF16),
        grid_spec=pltpu.PrefetchScalarGridSpec(
            num_scalar_prefetch=6, grid=(n_blocks,),
            in_specs=[blk_row(D_MODEL), blk_row(LANES)],
            out_specs=pl.BlockSpec(memory_space=pl.ANY),
            scratch_shapes=[pltpu.VMEM((LOCAL_SLOTS, D_MODEL), BF16),
                            pltpu.VMEM((FFN_TM, D_MODEL), BF16),
                            pltpu.SemaphoreType.DMA, pltpu.SemaphoreType.DMA]),
        compiler_params=_cparams(1), name="moe_dispatch",
    )(*tabs, zrow, zflag, x2d, route)

    w_spec = lambda r, c: pl.BlockSpec((None, None, r, c), lambda i, te, nu: (layer, te[i], 0, 0))
    slot_blk = pl.BlockSpec((FFN_TM, D_MODEL), lambda i, te, nu: (i, 0))
    ys = pl.pallas_call(
        _ffn_kernel,
        out_shape=jax.ShapeDtypeStruct((n_slots, D_MODEL), BF16),
        grid_spec=pltpu.PrefetchScalarGridSpec(
            num_scalar_prefetch=2, grid=(n_tiles,),
            in_specs=[slot_blk, w_spec(D_MODEL, MOE_D_EXPERT), w_spec(D_MODEL, MOE_D_EXPERT),
                      w_spec(MOE_D_EXPERT, D_MODEL)],
            out_specs=slot_blk,
            scratch_shapes=[pltpu.VMEM((D_MODEL, MOE_D_EXPERT), BF16),
                            pltpu.VMEM((D_MODEL, MOE_D_EXPERT), BF16),
                            pltpu.VMEM((MOE_D_EXPERT, D_MODEL), BF16)]),
        compiler_params=_cparams(1), name="moe_ffn",
    )(tile_expert, n_used, xs, w_gate, w_up, w_down)

    vec = pl.BlockSpec((1, D_MODEL), lambda i, *_: (0, 0))
    return pl.pallas_call(
        _combine_kernel,
        out_shape=jax.ShapeDtypeStruct((t, D_MODEL), F32),
        grid_spec=pltpu.PrefetchScalarGridSpec(
            num_scalar_prefetch=4, grid=(n_blocks,),
            in_specs=[pl.BlockSpec(memory_space=pl.ANY), blk_row(LANES), blk_row(D_MODEL), vec, vec],
            out_specs=blk_row(D_MODEL),
            scratch_shapes=[pltpu.VMEM((LOCAL_SLOTS, D_MODEL), BF16), pltpu.SemaphoreType.DMA]),
        compiler_params=_cparams(1), name="moe_combine",
    )(*tabs, ys, route, x2d, ln_g.reshape(1, -1), ln_b.reshape(1, -1))


def kernel(x, ssd_w_in, ssd_conv_w, ssd_conv_b, ssd_dt_bias, ssd_a_log, ssd_d, ssd_norm_w, ssd_w_out,
           sc_w_in, sc_conv_w, sc_w_out, att_w_qkv, att_w_out,
           moe_wg, moe_bg, moe_we, moe_be, moe_w_gate, moe_w_up, moe_w_down, ln_g, ln_b):
    bsz, seq, d = x.shape
    assert d == D_MODEL
    h = x.reshape(bsz * seq, d)
    for i in range(DEPTH):
        kind, j = i % N_MIXERS, i // N_MIXERS
        if kind == 0:
            h = _ssd_mixer(h, bsz, seq, ssd_w_in[j], ssd_conv_w[j], ssd_conv_b[j], ssd_dt_bias[j],
                           ssd_a_log[j], ssd_d[j], ssd_norm_w[j], ssd_w_out[j], ln_g[i, 0], ln_b[i, 0])
        elif kind == 1:
            h = _sconv_mixer(h, bsz, seq, sc_w_in[j], sc_conv_w[j], sc_w_out[j], ln_g[i, 0], ln_b[i, 0])
        else:
            h = _attn_mixer(h, bsz, seq, att_w_qkv[j], att_w_out[j], ln_g[i, 0], ln_b[i, 0])
        h = _moe_layer(h, i, moe_wg[i], moe_bg[i], moe_we[i], moe_be[i],
                       moe_w_gate, moe_w_up, moe_w_down, ln_g[i, 1], ln_b[i, 1])
    return h.reshape(bsz, seq, d)
```

```python
import functools

import numpy as np
import jax
import jax.numpy as jnp
from jax import lax
from jax.experimental import pallas as pl
from jax.experimental.pallas import tpu as pltpu

D_MODEL = 1024
DEPTH = 4
N_MIXERS = 3
DEEPNORM_ALPHA = (2 * DEPTH) ** 0.25
LN_EPS = 1e-5

SSD_D_INNER = 2 * D_MODEL
SSD_HEAD_DIM = 64
SSD_N_HEADS = SSD_D_INNER // SSD_HEAD_DIM
SSD_N_GROUPS = 4
SSD_D_STATE = 128
SSD_CONV_WIDTH = 4
SSD_CHUNK = 256
SSD_SUBCHUNK = 128
SSD_CONV_DIM = SSD_D_INNER + 2 * SSD_N_GROUPS * SSD_D_STATE
SSD_HEADS_PER_GROUP = SSD_N_HEADS // SSD_N_GROUPS
SSD_GROUP_WIDTH = SSD_D_INNER // SSD_N_GROUPS

SC_WIDTH = 3

ATT_GROUPS = ((128, 1), (512, 4), (2048, 16))
ATT_HEAD_DIM = 64
ATT_HEADS = D_MODEL // ATT_HEAD_DIM
ATT_BLOCK = 128
ATT_COLS = ATT_HEADS * ATT_HEAD_DIM

MOE_GROUPS = 4
MOE_EXPERTS_PER_GROUP = 8
MOE_N_EXPERTS = MOE_GROUPS * MOE_EXPERTS_PER_GROUP
MOE_TOP_K = 2
MOE_D_EXPERT = 256

LANES = 128
SUBLANES = 8
VMEM_LIMIT_BYTES = 56 * 1024 * 1024

PROJ_TM = 512
CONV_TM = 512
MERGE_TM = 512
ROUTE_TM = 512
FFN_TM = 512
COL_CHUNK = 512
ROW_TILE = D_MODEL // LANES

F32 = jnp.float32
BF16 = jnp.bfloat16
HIGHEST = lax.Precision.HIGHEST


def _cparams(n_axes):
    return pltpu.CompilerParams(dimension_semantics=("arbitrary",) * n_axes,
                                vmem_limit_bytes=VMEM_LIMIT_BYTES)


def _resident(shape):
    nd = len(shape)
    return pl.BlockSpec(shape, lambda *_: (0,) * nd, pipeline_mode=pl.Buffered(1))


def _silu(v):
    return v * (1.0 / (1.0 + jnp.exp(-v)))


def _softplus(v):
    return jnp.maximum(v, 0.0) + jnp.log1p(jnp.exp(-jnp.abs(v)))


def _layernorm(v, g, b):
    mu = jnp.mean(v, axis=-1, keepdims=True)
    d = v - mu
    var = jnp.mean(d * d, axis=-1, keepdims=True)
    return d * lax.rsqrt(var + LN_EPS) * g + b


def _split_hi_lo(v):
    hi = v.astype(BF16)
    lo = (v - hi.astype(F32)).astype(BF16)
    return jnp.concatenate([hi, lo], axis=1)


def _expansion_matrix(n_heads, width):
    e = np.zeros((2 * LANES, n_heads * width), np.float32)
    for h in range(n_heads):
        e[h, h * width:(h + 1) * width] = 1.0
        e[LANES + h, h * width:(h + 1) * width] = 1.0
    return jnp.asarray(e, dtype=BF16)


def _dot_3pass(x, w):
    xh = x.astype(BF16)
    xl = (x - xh.astype(F32)).astype(BF16)
    wh = w.astype(BF16)
    wl = (w - wh.astype(F32)).astype(BF16)
    return (jnp.dot(xh, wh, preferred_element_type=F32) + jnp.dot(xl, wh, preferred_element_type=F32)
            + jnp.dot(xh, wl, preferred_element_type=F32))


def _proj_kernel(x_ref, w_ref, wdt_ref, *outs, widths):
    x = x_ref[...]
    xb = x.astype(BF16)
    off = 0
    for j, n in enumerate(widths):
        for c in range(n // COL_CHUNK):
            r = jnp.dot(xb, w_ref[:, off + c * COL_CHUNK: off + (c + 1) * COL_CHUNK],
                        preferred_element_type=F32)
            outs[j][:, c * COL_CHUNK:(c + 1) * COL_CHUNK] = r.astype(outs[j].dtype)
        off += n
    outs[len(widths)][...] = _dot_3pass(x, wdt_ref[...])


def _project(x, w_bf16, widths, w_dt, name):
    t, k = x.shape
    n_total = sum(widths)
    assert w_bf16.shape == (k, n_total) and t % PROJ_TM == 0
    row = lambda n: pl.BlockSpec((PROJ_TM, n), lambda i: (i, 0))
    return pl.pallas_call(
        functools.partial(_proj_kernel, widths=tuple(widths)),
        out_shape=[jax.ShapeDtypeStruct((t, n), BF16) for n in widths]
        + [jax.ShapeDtypeStruct((t, w_dt.shape[1]), F32)],
        grid=(t // PROJ_TM,),
        in_specs=[row(k), _resident((k, n_total)), _resident(w_dt.shape)],
        out_specs=[row(n) for n in widths] + [row(w_dt.shape[1])],
        compiler_params=_cparams(1), name=name)(x, w_bf16, w_dt)


def _ssd_kernel(z_ref, xbc_ref, dt_ref, x_ref, convw_ref, convb_ref, dtb_ref, alog_ref,
                dskip_ref, normw_ref, wout_ref, lng_ref, lnb_ref, e_ref,
                o_ref, state_ref, carry_ref, y_ref):
    L = SSD_CHUNK
    SUB = SSD_SUBCHUNK
    N = SSD_D_STATE
    GW = SSD_GROUP_WIDTH
    c_idx = pl.program_id(1)

    @pl.when(c_idx == 0)
    def _():
        state_ref[...] = jnp.zeros_like(state_ref)
        carry_ref[...] = jnp.zeros_like(carry_ref)

    u_b = xbc_ref[...]
    u = u_b.astype(F32)
    cw = convw_ref[...]
    row = lax.broadcasted_iota(jnp.int32, (L, L), 0)
    col = lax.broadcasted_iota(jnp.int32, (L, L), 1)
    carry = carry_ref[...]
    row8 = lax.broadcasted_iota(jnp.int32, (SUBLANES, SSD_CONV_DIM), 0)
    acc = convb_ref[...] + cw[SSD_CONV_WIDTH - 1:SSD_CONV_WIDTH, :] * u
    head = jnp.zeros((SUBLANES, SSD_CONV_DIM), F32)
    for k in range(1, SSD_CONV_WIDTH):
        wk = cw[SSD_CONV_WIDTH - 1 - k:SSD_CONV_WIDTH - k, :]
        shift_mat = jnp.where(row - col == k, 1.0, 0.0).astype(BF16)
        acc = acc + wk * jnp.dot(shift_mat, u_b, preferred_element_type=F32)
        head = head + wk * jnp.where(row8 < k, pltpu.roll(carry, k, 0), 0.0)
    carry_ref[...] = u[L - SUBLANES:, :]
    acc = jnp.concatenate([acc[:SUBLANES] + head, acc[SUBLANES:]], axis=0)
    act = _silu(acc)
    xs = act[:, :SSD_D_INNER]
    bmat = act[:, SSD_D_INNER:SSD_D_INNER + SSD_N_GROUPS * N]
    cmat = act[:, SSD_D_INNER + SSD_N_GROUPS * N:]

    dt = _softplus(dt_ref[...] + dtb_ref[...])
    a = -jnp.exp(alog_ref[...])
    dta = dt * a
    rs = lax.broadcasted_iota(jnp.int32, (SUB, SUB), 0)
    cs_i = lax.broadcasted_iota(jnp.int32, (SUB, SUB), 1)
    causal = rs >= cs_i
    causal_f = causal.astype(F32)
    e_mat = e_ref[...]
    lane = lax.broadcasted_iota(jnp.int32, (1, LANES), 1)
    low_half = lane < SSD_HEAD_DIM
    dskip = dskip_ref[...]

    for sc in range(L // SUB):
        rows = slice(sc * SUB, (sc + 1) * SUB)
        dt_s = dt[rows]
        cs = jnp.dot(causal_f, dta[rows], precision=HIGHEST, preferred_element_type=F32)
        cs_t = cs.T
        last = cs[SUB - 1:SUB, :]
        xs_s = xs[rows]
        ecs_x = jnp.dot(_split_hi_lo(jnp.exp(cs)), e_mat, preferred_element_type=F32)
        ws_x = jnp.dot(_split_hi_lo(jnp.exp(last - cs) * dt_s), e_mat, preferred_element_type=F32)
        dt_x = jnp.dot(_split_hi_lo(dt_s), e_mat, preferred_element_type=F32)
        elast_x = jnp.dot(_split_hi_lo(jnp.broadcast_to(jnp.exp(last), (SUBLANES, LANES))), e_mat,
                          preferred_element_type=F32)[0:1, :]
        xw_b = (xs_s * ws_x).astype(BF16)
        xdt_b = (xs_s * dt_x).astype(BF16)
        for g in range(SSD_N_GROUPS):
            bg = bmat[rows, g * N:(g + 1) * N]
            cg_b = cmat[rows, g * N:(g + 1) * N].astype(BF16)
            cb = lax.dot_general(cg_b, bg.astype(BF16), (((1,), (1,)), ((), ())),
                                 preferred_element_type=F32)
            s_old = state_ref[g]
            yoff = jnp.dot(cg_b, s_old.astype(BF16), preferred_element_type=F32)
            for pr in range(SSD_HEADS_PER_GROUP // 2):
                pair = g * (SSD_HEADS_PER_GROUP // 2) + pr
                cols = slice(pair * LANES, (pair + 1) * LANES)
                x_pair = xdt_b[:, cols]
                ys = []
                for half in range(2):
                    h = 2 * pair + half
                    seg = cs[:, h:h + 1] - cs_t[h:h + 1, :]
                    mix = cb * jnp.exp(jnp.where(causal, seg, -jnp.inf))
                    ys.append(jnp.dot(mix.astype(BF16), x_pair, preferred_element_type=F32))
                off_cols = slice(pr * LANES, (pr + 1) * LANES)
                y_ref[rows, cols] = (jnp.where(low_half, ys[0], ys[1])
                                     + yoff[:, off_cols] * ecs_x[:, cols] + dskip[:, cols] * xs_s[:, cols])
            upd = jnp.dot(bg.T.astype(BF16), xw_b[:, g * GW:(g + 1) * GW], preferred_element_type=F32)
            state_ref[g] = s_old * elast_x[:, g * GW:(g + 1) * GW] + upd

    y = y_ref[...] * _silu(z_ref[...].astype(F32))
    normw = normw_ref[...]
    yn_parts = []
    for g in range(SSD_N_GROUPS):
        yg = y[:, g * GW:(g + 1) * GW]
        ms = jnp.mean(yg * yg, axis=-1, keepdims=True)
        yn_parts.append((yg * lax.rsqrt(ms + LN_EPS) * normw[:, g * GW:(g + 1) * GW]).astype(BF16))
    yn = jnp.concatenate(yn_parts, axis=1)
    hproj = jnp.dot(yn, wout_ref[...], preferred_element_type=F32)
    o_ref[...] = _layernorm(DEEPNORM_ALPHA * x_ref[...] + hproj, lng_ref[...], lnb_ref[...])


def _ssd_mixer(x2d, bsz, seq, w_in, conv_w, conv_b, dt_bias, a_log, d_skip, norm_w, w_out,
               ln_g, ln_b):
    t = x2d.shape[0]
    nc = seq // SSD_CHUNK
    assert seq % SSD_CHUNK == 0
    n_main = SSD_D_INNER + SSD_CONV_DIM
    w_main = w_in[:, :n_main].astype(BF16)
    w_dt = jnp.pad(w_in[:, n_main:], ((0, 0), (0, LANES - SSD_N_HEADS)))
    z, xbc, dt_raw = _project(x2d, w_main, (SSD_D_INNER, SSD_CONV_DIM), w_dt, name="ssd_in_proj")
    pad = (0, LANES - SSD_N_HEADS)
    dtb = jnp.pad(dt_bias, pad).reshape(1, LANES)
    alog = jnp.pad(a_log, pad).reshape(1, LANES)
    dskip = jnp.repeat(d_skip, SSD_HEAD_DIM).reshape(1, SSD_D_INNER)
    L = SSD_CHUNK
    blk = lambda w: pl.BlockSpec((L, w), lambda b, c: (b * nc + c, 0))
    return pl.pallas_call(
        _ssd_kernel,
        out_shape=jax.ShapeDtypeStruct((t, D_MODEL), F32),
        grid=(bsz, nc),
        in_specs=[blk(SSD_D_INNER), blk(SSD_CONV_DIM), blk(LANES), blk(D_MODEL),
                  _resident((SSD_CONV_WIDTH, SSD_CONV_DIM)), _resident((1, SSD_CONV_DIM)),
                  _resident((1, LANES)), _resident((1, LANES)),
                  _resident((1, SSD_D_INNER)), _resident((1, SSD_D_INNER)),
                  _resident((SSD_D_INNER, D_MODEL)),
                  _resident((1, D_MODEL)), _resident((1, D_MODEL)),
                  _resident((2 * LANES, SSD_D_INNER))],
        out_specs=blk(D_MODEL),
        scratch_shapes=[pltpu.VMEM((SSD_N_GROUPS, SSD_D_STATE, SSD_GROUP_WIDTH), F32),
                        pltpu.VMEM((SUBLANES, SSD_CONV_DIM), F32),
                        pltpu.VMEM((L, SSD_D_INNER), F32)],
        compiler_params=_cparams(2), name="ssd_scan",
    )(z, xbc, dt_raw, x2d, conv_w, conv_b.reshape(1, -1), dtb, alog, dskip,
      norm_w.reshape(1, -1), w_out.astype(BF16), ln_g.reshape(1, -1), ln_b.reshape(1, -1),
      _expansion_matrix(SSD_N_HEADS, SSD_HEAD_DIM))


def _sconv_kernel(x_ref, win_ref, convw_ref, wout_ref, lng_ref, lnb_ref, o_ref, vbuf_ref):
    tm = CONV_TM

    @pl.when(pl.program_id(1) == 0)
    def _():
        vbuf_ref[0:SUBLANES, :] = jnp.zeros((SUBLANES, D_MODEL), F32)

    x = x_ref[...]
    xb = x.astype(BF16)
    gb = jnp.dot(xb, win_ref[:, 0:D_MODEL], preferred_element_type=F32)
    gc = jnp.dot(xb, win_ref[:, D_MODEL:2 * D_MODEL], preferred_element_type=F32)
    u = jnp.dot(xb, win_ref[:, 2 * D_MODEL:3 * D_MODEL], preferred_element_type=F32)
    v = gc * u
    vbuf_ref[SUBLANES:SUBLANES + tm, :] = v
    cw = convw_ref[...]
    conv = cw[SC_WIDTH - 1:SC_WIDTH, :] * v
    for k in range(SC_WIDTH - 1):
        shift = SC_WIDTH - 1 - k
        conv = conv + cw[k:k + 1, :] * vbuf_ref[SUBLANES - shift:SUBLANES - shift + tm, :]
    vbuf_ref[0:SUBLANES, :] = vbuf_ref[tm:tm + SUBLANES, :]
    y = (gb * conv).astype(BF16)
    hproj = jnp.dot(y, wout_ref[...], preferred_element_type=F32)
    o_ref[...] = _layernorm(DEEPNORM_ALPHA * x + hproj, lng_ref[...], lnb_ref[...])


def _sconv_mixer(x2d, bsz, seq, w_in, conv_w, w_out, ln_g, ln_b):
    t = x2d.shape[0]
    ns = seq // CONV_TM
    assert seq % CONV_TM == 0
    blk = pl.BlockSpec((CONV_TM, D_MODEL), lambda b, s: (b * ns + s, 0))
    return pl.pallas_call(
        _sconv_kernel,
        out_shape=jax.ShapeDtypeStruct((t, D_MODEL), F32),
        grid=(bsz, ns),
        in_specs=[blk, _resident((D_MODEL, 3 * D_MODEL)), _resident((SC_WIDTH, D_MODEL)),
                  _resident((D_MODEL, D_MODEL)), _resident((1, D_MODEL)), _resident((1, D_MODEL))],
        out_specs=blk,
        scratch_shapes=[pltpu.VMEM((CONV_TM + SUBLANES, D_MODEL), F32)],
        compiler_params=_cparams(2), name="sconv_mixer",
    )(x2d, w_in.astype(BF16), conv_w, w_out.astype(BF16), ln_g.reshape(1, -1), ln_b.reshape(1, -1))


ATT_TM = 512
QKV_COLS = 3 * ATT_COLS


def _stream_view(arr, dil):
    t, c = arr.shape
    span = ATT_BLOCK * dil
    per_tile = ATT_TM // dil
    tiles_per_span = span // ATT_TM
    view = (t // span, dil, ATT_BLOCK, c)
    spec = pl.BlockSpec((None, dil, per_tile, c),
                        lambda i: (i // tiles_per_span, 0, i % tiles_per_span, 0))
    return view, spec


def _deinterleave_rows(slab_ref, dil):
    n_i = ATT_TM // dil
    rows = [jnp.concatenate([slab_ref[s, pl.ds(r, n_i, stride=dil), :] for s in range(ROW_TILE)],
                            axis=1) for r in range(dil)]
    return jnp.concatenate(rows, axis=0).astype(BF16)


def _interleave_rows(slab_ref, blk_ref, dil):
    n_i, c = blk_ref.shape[1], blk_ref.shape[2]
    for r in range(dil):
        br = blk_ref[r].astype(F32)
        for s in range(c // LANES):
            slab_ref[s, pl.ds(r, n_i, stride=dil), :] = br[:, s * LANES:(s + 1) * LANES]
    return jnp.concatenate([slab_ref[s] for s in range(c // LANES)], axis=1)


def _qkv_proj_kernel(x_ref, w_ref, o0_ref, o1_ref, o2_ref, slab_ref):
    x = x_ref[...]
    for s in range(ROW_TILE):
        slab_ref[s] = x[:, s * LANES:(s + 1) * LANES]
    outs = (o0_ref, o1_ref, o2_ref)
    for gi, (_, dil) in enumerate(ATT_GROUPS):
        xb = x.astype(BF16) if dil == 1 else _deinterleave_rows(slab_ref, dil)
        for c in range(QKV_COLS // COL_CHUNK):
            col = gi * QKV_COLS + c * COL_CHUNK
            r = jnp.dot(xb, w_ref[:, col:col + COL_CHUNK], preferred_element_type=F32)
            dst = slice(c * COL_CHUNK, (c + 1) * COL_CHUNK)
            if dil == 1:
                outs[gi][:, dst] = r.astype(BF16)
            else:
                outs[gi][:, :, dst] = r.reshape(dil, ATT_TM // dil, COL_CHUNK).astype(BF16)


def _attn_kernel(q_ref, kp_ref, kc_ref, vp_ref, vc_ref, o_ref, lse_ref, *, dil):
    nq = ATT_BLOCK
    j = pl.program_id(1)
    q = q_ref[...] * (ATT_HEAD_DIM ** -0.5)
    kk = jnp.concatenate([kp_ref[...], kc_ref[...]], axis=0)
    vv = jnp.concatenate([vp_ref[...], vc_ref[...]], axis=0)
    qi = lax.broadcasted_iota(jnp.int32, (nq, 2 * nq), 0)
    kj = lax.broadcasted_iota(jnp.int32, (nq, 2 * nq), 1)
    dist = qi + nq - kj
    first_key = jnp.where(j >= dil, 0, nq)
    valid = (dist >= 0) & (dist <= nq) & (kj >= first_key)
    pen = jnp.where(valid, (dist * dil).astype(F32), jnp.inf)
    lane = lax.broadcasted_iota(jnp.int32, (1, LANES), 1)
    low_half = lane < ATT_HEAD_DIM
    lane_q = lax.broadcasted_iota(jnp.int32, (nq, LANES), 1)
    lse_all = jnp.zeros((nq, LANES), F32)
    zero_b = jnp.zeros((), BF16)
    for pair in range(ATT_HEADS // 2):
        sl = slice(pair * LANES, (pair + 1) * LANES)
        q_pair, k_pair, v_pair = q[:, sl], kk[:, sl], vv[:, sl]
        outs = []
        for half in range(2):
            h = 2 * pair + half
            slope = 2.0 ** (-8.0 * (h + 1) / ATT_HEADS)
            qm = jnp.where(low_half if half == 0 else jnp.logical_not(low_half), q_pair, zero_b)
            s = lax.dot_general(qm, k_pair, (((1,), (1,)), ((), ())), preferred_element_type=F32)
            s = s - slope * pen
            m = jnp.max(s, axis=-1, keepdims=True)
            p = jnp.exp(s - m)
            denom = jnp.sum(p, axis=-1, keepdims=True)
            pv = jnp.dot(p.astype(BF16), v_pair, preferred_element_type=F32)
            outs.append(pv * (1.0 / denom))
            lse_all = jnp.where(lane_q == h, m + jnp.log(denom), lse_all)
        o_ref[:, sl] = jnp.where(low_half, outs[0], outs[1]).astype(o_ref.dtype)
    lse_ref[...] = lse_all


def _attn_group(qkv_g, bsz, seq, gi, dil):
    t = qkv_g.shape[0]
    assert seq % (dil * ATT_BLOCK) == 0
    nblk = seq // ATT_BLOCK
    blk = (ATT_BLOCK, ATT_COLS)
    cur = lambda c: pl.BlockSpec(blk, lambda b, j: (b * nblk + j, c))
    prev = lambda c: pl.BlockSpec(blk, lambda b, j: (b * nblk + jnp.maximum(j - dil, 0), c))
    return pl.pallas_call(
        functools.partial(_attn_kernel, dil=dil),
        out_shape=[jax.ShapeDtypeStruct((t, ATT_COLS), BF16),
                   jax.ShapeDtypeStruct((t, LANES), F32)],
        grid=(bsz, nblk),
        in_specs=[cur(0), prev(1), cur(1), prev(2), cur(2)],
        out_specs=[pl.BlockSpec(blk, lambda b, j: (b * nblk + j, 0)),
                   pl.BlockSpec((ATT_BLOCK, LANES), lambda b, j: (b * nblk + j, 0))],
        compiler_params=_cparams(2), name=f"dil_attn_g{gi}",
    )(qkv_g, qkv_g, qkv_g, qkv_g, qkv_g)


def _attn_merge_kernel(o0_ref, o1_ref, o2_ref, l0_ref, l1_ref, l2_ref, x_ref, e_ref, wout_ref,
                       lng_ref, lnb_ref, out_ref, oslab_ref, lslab_ref):
    dils = [d for _, d in ATT_GROUPS]
    outs = [o0_ref[...].astype(F32)] + [_interleave_rows(oslab_ref, r, d)
                                        for r, d in ((o1_ref, dils[1]), (o2_ref, dils[2]))]
    l0 = l0_ref[...]
    l1 = _interleave_rows(lslab_ref, l1_ref, dils[1])
    l2 = _interleave_rows(lslab_ref, l2_ref, dils[2])
    m = jnp.maximum(jnp.maximum(l0, l1), l2)
    es = [jnp.exp(l0 - m), jnp.exp(l1 - m), jnp.exp(l2 - m)]
    den = es[0] + es[1] + es[2]
    e_mat = e_ref[...]
    acc = None
    for e, o in zip(es, outs):
        w_x = jnp.dot(_split_hi_lo(e / den), e_mat, preferred_element_type=F32)
        acc = w_x * o if acc is None else acc + w_x * o
    hproj = jnp.dot(acc.astype(BF16), wout_ref[...], preferred_element_type=F32)
    out_ref[...] = _layernorm(DEEPNORM_ALPHA * x_ref[...] + hproj, lng_ref[...], lnb_ref[...])


def _attn_mixer(x2d, bsz, seq, w_qkv, w_out, ln_g, ln_b):
    t = x2d.shape[0]
    n_groups = len(ATT_GROUPS)
    assert n_groups == 3 and ATT_GROUPS[0][1] == 1
    for window, dil in ATT_GROUPS:
        assert window // dil == ATT_BLOCK and seq % (ATT_BLOCK * dil) == 0
        assert ATT_TM % dil == 0 and (ATT_BLOCK * dil) % ATT_TM == 0 or dil == 1
    row = lambda w: pl.BlockSpec((ATT_TM, w), lambda i: (i, 0))

    views, specs = zip(*[_stream_view(jax.ShapeDtypeStruct((t, QKV_COLS), BF16), d)
                         for _, d in ATT_GROUPS[1:]])
    qkvs = pl.pallas_call(
        _qkv_proj_kernel,
        out_shape=[jax.ShapeDtypeStruct((t, QKV_COLS), BF16)]
        + [jax.ShapeDtypeStruct(v, BF16) for v in views],
        grid=(t // ATT_TM,),
        in_specs=[row(D_MODEL), _resident((D_MODEL, n_groups * QKV_COLS))],
        out_specs=[row(QKV_COLS)] + list(specs),
        scratch_shapes=[pltpu.VMEM((ROW_TILE, ATT_TM, LANES), F32)],
        compiler_params=_cparams(1), name="att_qkv_proj",
    )(x2d, w_qkv.astype(BF16))

    outs, lses = [], []
    for gi, (_, dil) in enumerate(ATT_GROUPS):
        o, l = _attn_group(qkvs[gi].reshape(t, QKV_COLS), bsz, seq, gi, dil)
        outs.append(o)
        lses.append(l)

    in_arrays, in_specs = [], []
    for group in (outs, lses):
        for arr, (_, dil) in zip(group, ATT_GROUPS):
            if dil == 1:
                in_arrays.append(arr)
                in_specs.append(row(arr.shape[1]))
            else:
                view, spec = _stream_view(arr, dil)
                in_arrays.append(arr.reshape(view))
                in_specs.append(spec)
    return pl.pallas_call(
        _attn_merge_kernel,
        out_shape=jax.ShapeDtypeStruct((t, D_MODEL), F32),
        grid=(t // ATT_TM,),
        in_specs=in_specs + [row(D_MODEL), _resident((2 * LANES, ATT_COLS)),
                             _resident((ATT_COLS, D_MODEL)),
                             _resident((1, D_MODEL)), _resident((1, D_MODEL))],
        out_specs=row(D_MODEL),
        scratch_shapes=[pltpu.VMEM((ATT_COLS // LANES, ATT_TM, LANES), F32),
                        pltpu.VMEM((1, ATT_TM, LANES), F32)],
        compiler_params=_cparams(1), name="att_merge",
    )(*in_arrays, x2d, _expansion_matrix(ATT_HEADS, ATT_HEAD_DIM), w_out.astype(BF16),
      ln_g.reshape(1, -1), ln_b.reshape(1, -1))


ROUTE_W1, ROUTE_W2, ROUTE_E1, ROUTE_E2, ROUTE_S1, ROUTE_S2 = range(6)
EXPERT_LANE0 = MOE_GROUPS
MOE_TB = 512
RUN_ALIGN = 16
LOCAL_SLOTS = 1536
assert LOCAL_SLOTS >= MOE_TOP_K * MOE_TB + MOE_N_EXPERTS * (RUN_ALIGN - 1)
assert ROUTE_TM == MOE_TB
assert MOE_GROUPS + MOE_N_EXPERTS <= LANES // 2


def _n_slot_tiles(t):
    n_blocks = t // MOE_TB
    worst = t * MOE_TOP_K + n_blocks * MOE_N_EXPERTS * (RUN_ALIGN - 1) + MOE_N_EXPERTS * (FFN_TM - 1)
    return -(-worst // FFN_TM), (t * MOE_TOP_K) // FFN_TM


def _router_kernel(x_ref, wr_ref, br_ref, route_ref, cnt_ref, wpk_ref):
    tm = ROUTE_TM
    half = LANES // 2

    @pl.when(pl.program_id(0) == 0)
    def _():
        w = wr_ref[...]
        wh = w.astype(BF16).astype(F32)
        wl = (w - wh).astype(BF16).astype(F32)
        wpk_ref[...] = (wh + pltpu.roll(wl, half, 1)).astype(BF16)

    x = x_ref[...]
    xh = x.astype(BF16)
    xl = (x - xh.astype(F32)).astype(BF16)
    w_packed = wpk_ref[...]
    prod = (jnp.dot(xh, w_packed, preferred_element_type=F32)
            + jnp.dot(xl, w_packed, preferred_element_type=F32))
    logits = prod + pltpu.roll(prod, half, 1) + br_ref[...]
    lane_i = lax.broadcasted_iota(jnp.int32, (tm, LANES), 1)
    lane = lane_i.astype(F32)
    big = float(LANES)

    def first_argmax(vals):
        vmax = jnp.max(vals, axis=-1, keepdims=True)
        idx = jnp.min(jnp.where(vals == vmax, lane, big), axis=-1, keepdims=True)
        return vmax, idx

    gl = jnp.where(lane_i < MOE_GROUPS, logits, -jnp.inf)
    gmax, gsel = first_argmax(gl)
    gw = 1.0 / jnp.sum(jnp.exp(gl - gmax), axis=-1, keepdims=True)
    grp_of_lane = jnp.right_shift(lane_i - EXPERT_LANE0, 3).astype(F32)
    in_grp = ((lane_i >= EXPERT_LANE0) & (lane_i < EXPERT_LANE0 + MOE_N_EXPERTS)
              & (grp_of_lane == gsel))
    el = jnp.where(in_grp, logits, -jnp.inf)
    v1, i1 = first_argmax(el)
    v2, i2 = first_argmax(jnp.where(lane == i1, -jnp.inf, el))
    e2 = jnp.exp(v2 - v1)
    w1 = gw / (1.0 + e2)
    w2 = gw * e2 / (1.0 + e2)

    sel1, sel2 = lane == i1, lane == i2
    onehot = jnp.where(sel1 | sel2, 1.0, 0.0)
    r_i = lax.broadcasted_iota(jnp.int32, (tm, tm), 0)
    c_i = lax.broadcasted_iota(jnp.int32, (tm, tm), 1)
    strict = jnp.where(c_i < r_i, 1.0, 0.0).astype(BF16)
    before = jnp.dot(strict, onehot.astype(BF16), preferred_element_type=F32)
    cnt = jnp.sum(onehot, axis=0, keepdims=True)
    run_tiles = jnp.floor((cnt + (RUN_ALIGN - 1)) * (1.0 / RUN_ALIGN))
    e_r = lax.broadcasted_iota(jnp.int32, (LANES, LANES), 0)
    e_c = lax.broadcasted_iota(jnp.int32, (LANES, LANES), 1)
    earlier = jnp.where(e_r < e_c, 1.0, 0.0).astype(BF16)
    run_start = RUN_ALIGN * jnp.dot(jnp.broadcast_to(run_tiles, (SUBLANES, LANES)).astype(BF16),
                                    earlier, preferred_element_type=F32)[0:1, :]
    slot = before + run_start
    slot1 = jnp.sum(jnp.where(sel1, slot, 0.0), axis=-1, keepdims=True)
    slot2 = jnp.sum(jnp.where(sel2, slot, 0.0), axis=-1, keepdims=True)
    cnt_ref[...] = jnp.broadcast_to(cnt, (SUBLANES, LANES))

    out = jnp.zeros((tm, LANES), F32)
    for pos, val in ((ROUTE_W1, w1), (ROUTE_W2, w2), (ROUTE_E1, i1 - EXPERT_LANE0),
                     (ROUTE_E2, i2 - EXPERT_LANE0), (ROUTE_S1, slot1), (ROUTE_S2, slot2)):
        out = jnp.where(lane_i == pos, val, out)
    route_ref[...] = out


def _start_run_copies(tabs, blk, make_copy):
    lstart_ref, gstart_ref, npiece_ref, _ = tabs
    buf = blk % 2

    def per_expert(e, carry):
        idx = blk * MOE_N_EXPERTS + e
        l0, g0 = lstart_ref[idx], gstart_ref[idx]

        def per_piece(j, c):
            make_copy(buf, pl.multiple_of(l0 + j * RUN_ALIGN, RUN_ALIGN),
                      pl.multiple_of(g0 + j * RUN_ALIGN, RUN_ALIGN)).start()
            return c

        return lax.fori_loop(0, npiece_ref[idx], per_piece, carry)

    lax.fori_loop(0, MOE_N_EXPERTS, per_expert, 0)


def _wait_run_copies(tabs, blk, make_copy):
    buf = blk % 2

    def wait_one(j, c):
        make_copy(buf, 0, 0).wait()
        return c

    lax.fori_loop(0, tabs[3][blk], wait_one, 0)


def _dispatch_kernel(lstart_ref, gstart_ref, npiece_ref, ntot_ref, zrow_ref, zflag_ref,
                     x_ref, route_ref, xs_hbm, loc_ref, zero_ref, sem, zsem):
    i = pl.program_id(0)
    n_zero = zrow_ref.shape[0]

    @pl.when(i == 0)
    def _():
        zero_ref[...] = jnp.zeros_like(zero_ref)

        def zcopy(e):
            start = pl.multiple_of(zrow_ref[e], FFN_TM)
            return pltpu.make_async_copy(zero_ref, xs_hbm.at[pl.ds(start, FFN_TM), :], zsem)

        def zstart(e, carry):
            @pl.when(zflag_ref[e] > 0)
            def _():
                zcopy(e).start()
            return carry

        def zwait(e, carry):
            @pl.when(zflag_ref[e] > 0)
            def _():
                zcopy(e).wait()
            return carry

        lax.fori_loop(0, n_zero, zstart, 0)
        lax.fori_loop(0, n_zero, zwait, 0)

    slots_t = route_ref[...].T
    s1 = slots_t[ROUTE_S1:ROUTE_S1 + 1, :]
    s2 = slots_t[ROUTE_S2:ROUTE_S2 + 1, :]
    xb = x_ref[...].astype(BF16)
    piece = MOE_TB
    buf = i % 2
    for p in range(LOCAL_SLOTS // piece):
        srow = (lax.broadcasted_iota(jnp.int32, (piece, MOE_TB), 0) + p * piece).astype(F32)
        perm = jnp.where((srow == s1) | (srow == s2), 1.0, 0.0).astype(BF16)
        loc_ref[buf, p * piece:(p + 1) * piece, :] = jnp.dot(
            perm, xb, preferred_element_type=F32).astype(BF16)

    def make_copy(b, l, g):
        return pltpu.make_async_copy(loc_ref.at[b, pl.ds(l, RUN_ALIGN), :],
                                     xs_hbm.at[pl.ds(g, RUN_ALIGN), :], sem.at[b])

    tabs = (lstart_ref, gstart_ref, npiece_ref, ntot_ref)
    _start_run_copies(tabs, i, make_copy)

    @pl.when(i > 0)
    def _():
        _wait_run_copies(tabs, i - 1, make_copy)

    @pl.when(i == pl.num_programs(0) - 1)
    def _():
        _wait_run_copies(tabs, i, make_copy)


def _ffn_kernel(te_ref, nused_ref, xs_ref, wg_ref, wu_ref, wd_ref, ys_ref, wgb_ref, wub_ref, wdb_ref):
    i = pl.program_id(0)

    @pl.when((i == 0) | (te_ref[i] != te_ref[jnp.maximum(i - 1, 0)]))
    def _():
        wgb_ref[...] = wg_ref[...].astype(BF16)
        wub_ref[...] = wu_ref[...].astype(BF16)
        wdb_ref[...] = wd_ref[...].astype(BF16)

    @pl.when(i < nused_ref[0])
    def _():
        x = xs_ref[...]
        g = jnp.dot(x, wgb_ref[...], preferred_element_type=F32)
        u = jnp.dot(x, wub_ref[...], preferred_element_type=F32)
        hid = (_silu(g) * u).astype(BF16)
        ys_ref[...] = jnp.dot(hid, wdb_ref[...], preferred_element_type=F32).astype(ys_ref.dtype)

    @pl.when(i >= nused_ref[0])
    def _():
        ys_ref[...] = jnp.zeros_like(ys_ref)


def _combine_kernel(lstart_ref, gstart_ref, npiece_ref, ntot_ref, ys_hbm, route_ref, x_ref,
                    lng_ref, lnb_ref, o_ref, loc_ref, sem):
    i = pl.program_id(0)

    def make_copy(b, l, g):
        return pltpu.make_async_copy(ys_hbm.at[pl.ds(g, RUN_ALIGN), :],
                                     loc_ref.at[b, pl.ds(l, RUN_ALIGN), :], sem.at[b])

    tabs = (lstart_ref, gstart_ref, npiece_ref, ntot_ref)

    @pl.when(i == 0)
    def _():
        loc_ref[...] = jnp.zeros_like(loc_ref)
        _start_run_copies(tabs, i, make_copy)

    @pl.when(i + 1 < pl.num_programs(0))
    def _():
        _start_run_copies(tabs, i + 1, make_copy)

    _wait_run_copies(tabs, i, make_copy)

    route = route_ref[...]
    w1, w2 = route[:, ROUTE_W1:ROUTE_W1 + 1], route[:, ROUTE_W2:ROUTE_W2 + 1]
    s1, s2 = route[:, ROUTE_S1:ROUTE_S1 + 1], route[:, ROUTE_S2:ROUTE_S2 + 1]
    piece = MOE_TB
    buf = i % 2
    f = None
    for p in range(LOCAL_SLOTS // piece):
        scol = (lax.broadcasted_iota(jnp.int32, (MOE_TB, piece), 1) + p * piece).astype(F32)
        comb = (jnp.where(scol == s1, w1, 0.0) + jnp.where(scol == s2, w2, 0.0)).astype(BF16)
        part = jnp.dot(comb, loc_ref[buf, p * piece:(p + 1) * piece, :], preferred_element_type=F32)
        f = part if f is None else f + part
    o_ref[...] = _layernorm(DEEPNORM_ALPHA * x_ref[...] + f, lng_ref[...], lnb_ref[...])


def _moe_layer(x2d, layer, wg, bg, we, be, w_gate, w_up, w_down, ln_g, ln_b):
    t = x2d.shape[0]
    assert t % MOE_TB == 0
    n_blocks = t // MOE_TB
    n_tiles, min_tiles = _n_slot_tiles(t)
    n_slots = n_tiles * FFN_TM

    pad_cols = LANES - MOE_GROUPS - MOE_N_EXPERTS
    wr = jnp.pad(jnp.concatenate([wg, we], axis=1), ((0, 0), (0, pad_cols)))
    br = jnp.pad(jnp.concatenate([bg, be]), (0, pad_cols)).reshape(1, LANES)
    route, counts = pl.pallas_call(
        _router_kernel,
        out_shape=[jax.ShapeDtypeStruct((t, LANES), F32),
                   jax.ShapeDtypeStruct((n_blocks * SUBLANES, LANES), F32)],
        grid=(n_blocks,),
        in_specs=[pl.BlockSpec((ROUTE_TM, D_MODEL), lambda i: (i, 0)),
                  _resident((D_MODEL, LANES)), _resident((1, LANES))],
        out_specs=[pl.BlockSpec((ROUTE_TM, LANES), lambda i: (i, 0)),
                   pl.BlockSpec((SUBLANES, LANES), lambda i: (i, 0))],
        scratch_shapes=[pltpu.VMEM((D_MODEL, LANES), BF16)],
        compiler_params=_cparams(1), name="moe_router",
    )(x2d, wr, br)

    cnt = counts.reshape(n_blocks, SUBLANES, LANES)[:, 0, EXPERT_LANE0:EXPERT_LANE0 + MOE_N_EXPERTS]
    cnt = cnt.astype(jnp.int32)
    run = ((cnt + RUN_ALIGN - 1) // RUN_ALIGN) * RUN_ALIGN
    lstart = jnp.cumsum(run, axis=1) - run
    seg_len = jnp.sum(run, axis=0)
    seg_pad = ((seg_len + FFN_TM - 1) // FFN_TM) * FFN_TM
    ends = jnp.cumsum(seg_pad)
    gstart = (ends - seg_pad)[None, :] + jnp.cumsum(run, axis=0) - run
    npiece = run // RUN_ALIGN
    tabs = [a.astype(jnp.int32).reshape(-1) for a in (lstart, gstart, npiece)]
    tabs.append(jnp.sum(npiece, axis=1).astype(jnp.int32))
    tile_start = jnp.arange(n_tiles, dtype=jnp.int32) * FFN_TM
    tile_expert = jnp.minimum(jnp.sum(ends[None, :] <= tile_start[:, None], axis=1),
                              MOE_N_EXPERTS - 1).astype(jnp.int32)
    n_used = (ends[-1] // FFN_TM).astype(jnp.int32).reshape(1)
    tail_tile = jnp.arange(min_tiles, n_tiles, dtype=jnp.int32)
    zrow = jnp.concatenate([jnp.maximum(ends - FFN_TM, 0), tail_tile * FFN_TM]).astype(jnp.int32)
    zflag = jnp.concatenate([seg_pad > 0, tail_tile >= n_used[0]]).astype(jnp.int32)

    blk_row = lambda w: pl.BlockSpec((MOE_TB, w), lambda i, *_: (i, 0))
    xs = pl.pallas_call(
        _dispatch_kernel,
        out_shape=jax.ShapeDtypeStruct((n_slots, D_MODEL), BF16),
        grid_spec=pltpu.PrefetchScalarGridSpec(
            num_scalar_prefetch=6, grid=(n_blocks,),
            in_specs=[blk_row(D_MODEL), blk_row(LANES)],
            out_specs=pl.BlockSpec(memory_space=pl.ANY),
            scratch_shapes=[pltpu.VMEM((2, LOCAL_SLOTS, D_MODEL), BF16),
                            pltpu.VMEM((FFN_TM, D_MODEL), BF16),
                            pltpu.SemaphoreType.DMA((2,)), pltpu.SemaphoreType.DMA]),
        compiler_params=_cparams(1), name="moe_dispatch",
    )(*tabs, zrow, zflag, x2d, route)

    w_spec = lambda r, c: pl.BlockSpec((None, None, r, c), lambda i, te, nu: (layer, te[i], 0, 0))
    slot_blk = pl.BlockSpec((FFN_TM, D_MODEL), lambda i, te, nu: (i, 0))
    used_blk = pl.BlockSpec((FFN_TM, D_MODEL), lambda i, te, nu: (jnp.minimum(i, nu[0] - 1), 0))
    ys = pl.pallas_call(
        _ffn_kernel,
        out_shape=jax.ShapeDtypeStruct((n_slots, D_MODEL), BF16),
        grid_spec=pltpu.PrefetchScalarGridSpec(
            num_scalar_prefetch=2, grid=(n_tiles,),
            in_specs=[used_blk, w_spec(D_MODEL, MOE_D_EXPERT), w_spec(D_MODEL, MOE_D_EXPERT),
                      w_spec(MOE_D_EXPERT, D_MODEL)],
            out_specs=slot_blk,
            scratch_shapes=[pltpu.VMEM((D_MODEL, MOE_D_EXPERT), BF16),
                            pltpu.VMEM((D_MODEL, MOE_D_EXPERT), BF16),
                            pltpu.VMEM((MOE_D_EXPERT, D_MODEL), BF16)]),
        compiler_params=_cparams(1), name="moe_ffn",
    )(tile_expert, n_used, xs, w_gate, w_up, w_down)

    vec = pl.BlockSpec((1, D_MODEL), lambda i, *_: (0, 0))
    return pl.pallas_call(
        _combine_kernel,
        out_shape=jax.ShapeDtypeStruct((t, D_MODEL), F32),
        grid_spec=pltpu.PrefetchScalarGridSpec(
            num_scalar_prefetch=4, grid=(n_blocks,),
            in_specs=[pl.BlockSpec(memory_space=pl.ANY), blk_row(LANES), blk_row(D_MODEL), vec, vec],
            out_specs=blk_row(D_MODEL),
            scratch_shapes=[pltpu.VMEM((2, LOCAL_SLOTS, D_MODEL), BF16),
                            pltpu.SemaphoreType.DMA((2,))]),
        compiler_params=_cparams(1), name="moe_combine",
    )(*tabs, ys, route, x2d, ln_g.reshape(1, -1), ln_b.reshape(1, -1))


def kernel(x, ssd_w_in, ssd_conv_w, ssd_conv_b, ssd_dt_bias, ssd_a_log, ssd_d, ssd_norm_w, ssd_w_out,
           sc_w_in, sc_conv_w, sc_w_out, att_w_qkv, att_w_out,
           moe_wg, moe_bg, moe_we, moe_be, moe_w_gate, moe_w_up, moe_w_down, ln_g, ln_b):
    bsz, seq, d = x.shape
    assert d == D_MODEL
    h = x.reshape(bsz * seq, d)
    for i in range(DEPTH):
        kind, j = i % N_MIXERS, i // N_MIXERS
        if kind == 0:
            h = _ssd_mixer(h, bsz, seq, ssd_w_in[j], ssd_conv_w[j], ssd_conv_b[j], ssd_dt_bias[j],
                           ssd_a_log[j], ssd_d[j], ssd_norm_w[j], ssd_w_out[j], ln_g[i, 0], ln_b[i, 0])
        elif kind == 1:
            h = _sconv_mixer(h, bsz, seq, sc_w_in[j], sc_conv_w[j], sc_w_out[j], ln_g[i, 0], ln_b[i, 0])
        else:
            h = _attn_mixer(h, bsz, seq, att_w_qkv[j], att_w_out[j], ln_g[i, 0], ln_b[i, 0])
        h = _moe_layer(h, i, moe_wg[i], moe_bg[i], moe_we[i], moe_be[i],
                       moe_w_gate, moe_w_up, moe_w_down, ln_g[i, 1], ln_b[i, 1])
    return h.reshape(bsz, seq, d)
```

```python
import functools

import numpy as np
import jax
import jax.numpy as jnp
from jax import lax
from jax.experimental import pallas as pl
from jax.experimental.pallas import tpu as pltpu

D_MODEL = 1024
DEPTH = 4
N_MIXERS = 3
DEEPNORM_ALPHA = (2 * DEPTH) ** 0.25
LN_EPS = 1e-5

SSD_D_INNER = 2 * D_MODEL
SSD_HEAD_DIM = 64
SSD_N_HEADS = SSD_D_INNER // SSD_HEAD_DIM
SSD_N_GROUPS = 4
SSD_D_STATE = 128
SSD_CONV_WIDTH = 4
SSD_CHUNK = 256
SSD_SUBCHUNK = 128
SSD_CONV_DIM = SSD_D_INNER + 2 * SSD_N_GROUPS * SSD_D_STATE
SSD_HEADS_PER_GROUP = SSD_N_HEADS // SSD_N_GROUPS
SSD_GROUP_WIDTH = SSD_D_INNER // SSD_N_GROUPS

SC_WIDTH = 3

ATT_GROUPS = ((128, 1), (512, 4), (2048, 16))
ATT_HEAD_DIM = 64
ATT_HEADS = D_MODEL // ATT_HEAD_DIM
ATT_BLOCK = 128
ATT_COLS = ATT_HEADS * ATT_HEAD_DIM

MOE_GROUPS = 4
MOE_EXPERTS_PER_GROUP = 8
MOE_N_EXPERTS = MOE_GROUPS * MOE_EXPERTS_PER_GROUP
MOE_TOP_K = 2
MOE_D_EXPERT = 256

LANES = 128
SUBLANES = 8
VMEM_LIMIT_BYTES = 56 * 1024 * 1024

PROJ_TM = 512
CONV_TM = 512
MERGE_TM = 512
ROUTE_TM = 512
FFN_TM = 512
COL_CHUNK = 512
ROW_TILE = D_MODEL // LANES

F32 = jnp.float32
BF16 = jnp.bfloat16
HIGHEST = lax.Precision.HIGHEST


def _cparams(n_axes):
    return pltpu.CompilerParams(dimension_semantics=("arbitrary",) * n_axes,
                                vmem_limit_bytes=VMEM_LIMIT_BYTES)


def _resident(shape):
    nd = len(shape)
    return pl.BlockSpec(shape, lambda *_: (0,) * nd, pipeline_mode=pl.Buffered(1))


def _silu(v):
    return v * (1.0 / (1.0 + jnp.exp(-v)))


def _softplus(v):
    return jnp.maximum(v, 0.0) + jnp.log1p(jnp.exp(-jnp.abs(v)))


def _layernorm(v, g, b):
    mu = jnp.mean(v, axis=-1, keepdims=True)
    d = v - mu
    var = jnp.mean(d * d, axis=-1, keepdims=True)
    return d * lax.rsqrt(var + LN_EPS) * g + b


def _split_hi_lo(v):
    hi = v.astype(BF16)
    lo = (v - hi.astype(F32)).astype(BF16)
    return jnp.concatenate([hi, lo], axis=1)


def _expansion_matrix(n_heads, width):
    e = np.zeros((2 * LANES, n_heads * width), np.float32)
    for h in range(n_heads):
        e[h, h * width:(h + 1) * width] = 1.0
        e[LANES + h, h * width:(h + 1) * width] = 1.0
    return jnp.asarray(e, dtype=BF16)


def _dot_3pass(x, w):
    xh = x.astype(BF16)
    xl = (x - xh.astype(F32)).astype(BF16)
    wh = w.astype(BF16)
    wl = (w - wh.astype(F32)).astype(BF16)
    return (jnp.dot(xh, wh, preferred_element_type=F32) + jnp.dot(xl, wh, preferred_element_type=F32)
            + jnp.dot(xh, wl, preferred_element_type=F32))


def _proj_kernel(x_ref, w_ref, wdt_ref, *outs, widths):
    x = x_ref[...]
    xb = x.astype(BF16)
    off = 0
    for j, n in enumerate(widths):
        for c in range(n // COL_CHUNK):
            r = jnp.dot(xb, w_ref[:, off + c * COL_CHUNK: off + (c + 1) * COL_CHUNK],
                        preferred_element_type=F32)
            outs[j][:, c * COL_CHUNK:(c + 1) * COL_CHUNK] = r.astype(outs[j].dtype)
        off += n
    outs[len(widths)][...] = _dot_3pass(x, wdt_ref[...])


def _project(x, w_bf16, widths, w_dt, name):
    t, k = x.shape
    n_total = sum(widths)
    assert w_bf16.shape == (k, n_total) and t % PROJ_TM == 0
    row = lambda n: pl.BlockSpec((PROJ_TM, n), lambda i: (i, 0))
    return pl.pallas_call(
        functools.partial(_proj_kernel, widths=tuple(widths)),
        out_shape=[jax.ShapeDtypeStruct((t, n), BF16) for n in widths]
        + [jax.ShapeDtypeStruct((t, w_dt.shape[1]), F32)],
        grid=(t // PROJ_TM,),
        in_specs=[row(k), _resident((k, n_total)), _resident(w_dt.shape)],
        out_specs=[row(n) for n in widths] + [row(w_dt.shape[1])],
        compiler_params=_cparams(1), name=name)(x, w_bf16, w_dt)


def _ssd_kernel(z_ref, xbc_ref, dt_ref, x_ref, convw_ref, convb_ref, dtb_ref, alog_ref,
                dskip_ref, normw_ref, wout_ref, lng_ref, lnb_ref, e_ref,
                o_ref, state_ref, carry_ref, y_ref):
    L = SSD_CHUNK
    SUB = SSD_SUBCHUNK
    N = SSD_D_STATE
    GW = SSD_GROUP_WIDTH
    c_idx = pl.program_id(1)

    @pl.when(c_idx == 0)
    def _():
        state_ref[...] = jnp.zeros_like(state_ref)
        carry_ref[...] = jnp.zeros_like(carry_ref)

    u_b = xbc_ref[...]
    u = u_b.astype(F32)
    cw = convw_ref[...]
    row = lax.broadcasted_iota(jnp.int32, (L, L), 0)
    col = lax.broadcasted_iota(jnp.int32, (L, L), 1)
    carry = carry_ref[...]
    row8 = lax.broadcasted_iota(jnp.int32, (SUBLANES, SSD_CONV_DIM), 0)
    acc = convb_ref[...] + cw[SSD_CONV_WIDTH - 1:SSD_CONV_WIDTH, :] * u
    head = jnp.zeros((SUBLANES, SSD_CONV_DIM), F32)
    for k in range(1, SSD_CONV_WIDTH):
        wk = cw[SSD_CONV_WIDTH - 1 - k:SSD_CONV_WIDTH - k, :]
        shift_mat = jnp.where(row - col == k, 1.0, 0.0).astype(BF16)
        acc = acc + wk * jnp.dot(shift_mat, u_b, preferred_element_type=F32)
        head = head + wk * jnp.where(row8 < k, pltpu.roll(carry, k, 0), 0.0)
    carry_ref[...] = u[L - SUBLANES:, :]
    acc = jnp.concatenate([acc[:SUBLANES] + head, acc[SUBLANES:]], axis=0)
    act = _silu(acc)
    xs = act[:, :SSD_D_INNER]
    bmat = act[:, SSD_D_INNER:SSD_D_INNER + SSD_N_GROUPS * N]
    cmat = act[:, SSD_D_INNER + SSD_N_GROUPS * N:]

    dt = _softplus(dt_ref[...] + dtb_ref[...])
    a = -jnp.exp(alog_ref[...])
    dta = dt * a
    rs = lax.broadcasted_iota(jnp.int32, (SUB, SUB), 0)
    cs_i = lax.broadcasted_iota(jnp.int32, (SUB, SUB), 1)
    causal = rs >= cs_i
    causal_f = causal.astype(F32)
    e_mat = e_ref[...]
    lane = lax.broadcasted_iota(jnp.int32, (1, LANES), 1)
    low_half = lane < SSD_HEAD_DIM
    dskip = dskip_ref[...]

    for sc in range(L // SUB):
        rows = slice(sc * SUB, (sc + 1) * SUB)
        dt_s = dt[rows]
        cs = jnp.dot(causal_f, dta[rows], precision=HIGHEST, preferred_element_type=F32)
        cs_t = cs.T
        last = cs[SUB - 1:SUB, :]
        xs_s = xs[rows]
        ecs_x = jnp.dot(_split_hi_lo(jnp.exp(cs)), e_mat, preferred_element_type=F32)
        ws_x = jnp.dot(_split_hi_lo(jnp.exp(last - cs) * dt_s), e_mat, preferred_element_type=F32)
        dt_x = jnp.dot(_split_hi_lo(dt_s), e_mat, preferred_element_type=F32)
        elast_x = jnp.dot(_split_hi_lo(jnp.broadcast_to(jnp.exp(last), (SUBLANES, LANES))), e_mat,
                          preferred_element_type=F32)[0:1, :]
        xw_b = (xs_s * ws_x).astype(BF16)
        xdt_b = (xs_s * dt_x).astype(BF16)
        for g in range(SSD_N_GROUPS):
            bg = bmat[rows, g * N:(g + 1) * N]
            cg_b = cmat[rows, g * N:(g + 1) * N].astype(BF16)
            cb = lax.dot_general(cg_b, bg.astype(BF16), (((1,), (1,)), ((), ())),
                                 preferred_element_type=F32)
            s_old = state_ref[g]
            yoff = jnp.dot(cg_b, s_old.astype(BF16), preferred_element_type=F32)
            for pr in range(SSD_HEADS_PER_GROUP // 2):
                pair = g * (SSD_HEADS_PER_GROUP // 2) + pr
                cols = slice(pair * LANES, (pair + 1) * LANES)
                x_pair = xdt_b[:, cols]
                ys = []
                for half in range(2):
                    h = 2 * pair + half
                    seg = cs[:, h:h + 1] - cs_t[h:h + 1, :]
                    mix = cb * jnp.exp(jnp.where(causal, seg, -jnp.inf))
                    ys.append(jnp.dot(mix.astype(BF16), x_pair, preferred_element_type=F32))
                off_cols = slice(pr * LANES, (pr + 1) * LANES)
                y_ref[rows, cols] = (jnp.where(low_half, ys[0], ys[1])
                                     + yoff[:, off_cols] * ecs_x[:, cols] + dskip[:, cols] * xs_s[:, cols])
            upd = jnp.dot(bg.T.astype(BF16), xw_b[:, g * GW:(g + 1) * GW], preferred_element_type=F32)
            state_ref[g] = s_old * elast_x[:, g * GW:(g + 1) * GW] + upd

    y = y_ref[...] * _silu(z_ref[...].astype(F32))
    normw = normw_ref[...]
    yn_parts = []
    for g in range(SSD_N_GROUPS):
        yg = y[:, g * GW:(g + 1) * GW]
        ms = jnp.mean(yg * yg, axis=-1, keepdims=True)
        yn_parts.append((yg * lax.rsqrt(ms + LN_EPS) * normw[:, g * GW:(g + 1) * GW]).astype(BF16))
    yn = jnp.concatenate(yn_parts, axis=1)
    hproj = jnp.dot(yn, wout_ref[...], preferred_element_type=F32)
    o_ref[...] = _layernorm(DEEPNORM_ALPHA * x_ref[...] + hproj, lng_ref[...], lnb_ref[...])


def _ssd_mixer(x2d, bsz, seq, w_in, conv_w, conv_b, dt_bias, a_log, d_skip, norm_w, w_out,
               ln_g, ln_b):
    t = x2d.shape[0]
    nc = seq // SSD_CHUNK
    assert seq % SSD_CHUNK == 0
    n_main = SSD_D_INNER + SSD_CONV_DIM
    w_main = w_in[:, :n_main].astype(BF16)
    w_dt = jnp.pad(w_in[:, n_main:], ((0, 0), (0, LANES - SSD_N_HEADS)))
    z, xbc, dt_raw = _project(x2d, w_main, (SSD_D_INNER, SSD_CONV_DIM), w_dt, name="ssd_in_proj")
    pad = (0, LANES - SSD_N_HEADS)
    dtb = jnp.pad(dt_bias, pad).reshape(1, LANES)
    alog = jnp.pad(a_log, pad).reshape(1, LANES)
    dskip = jnp.repeat(d_skip, SSD_HEAD_DIM).reshape(1, SSD_D_INNER)
    L = SSD_CHUNK
    blk = lambda w: pl.BlockSpec((L, w), lambda b, c: (b * nc + c, 0))
    return pl.pallas_call(
        _ssd_kernel,
        out_shape=jax.ShapeDtypeStruct((t, D_MODEL), F32),
        grid=(bsz, nc),
        in_specs=[blk(SSD_D_INNER), blk(SSD_CONV_DIM), blk(LANES), blk(D_MODEL),
                  _resident((SSD_CONV_WIDTH, SSD_CONV_DIM)), _resident((1, SSD_CONV_DIM)),
                  _resident((1, LANES)), _resident((1, LANES)),
                  _resident((1, SSD_D_INNER)), _resident((1, SSD_D_INNER)),
                  _resident((SSD_D_INNER, D_MODEL)),
                  _resident((1, D_MODEL)), _resident((1, D_MODEL)),
                  _resident((2 * LANES, SSD_D_INNER))],
        out_specs=blk(D_MODEL),
        scratch_shapes=[pltpu.VMEM((SSD_N_GROUPS, SSD_D_STATE, SSD_GROUP_WIDTH), F32),
                        pltpu.VMEM((SUBLANES, SSD_CONV_DIM), F32),
                        pltpu.VMEM((L, SSD_D_INNER), F32)],
        compiler_params=_cparams(2), name="ssd_scan",
    )(z, xbc, dt_raw, x2d, conv_w, conv_b.reshape(1, -1), dtb, alog, dskip,
      norm_w.reshape(1, -1), w_out.astype(BF16), ln_g.reshape(1, -1), ln_b.reshape(1, -1),
      _expansion_matrix(SSD_N_HEADS, SSD_HEAD_DIM))


def _sconv_kernel(x_ref, win_ref, convw_ref, wout_ref, lng_ref, lnb_ref, o_ref, vbuf_ref):
    tm = CONV_TM

    @pl.when(pl.program_id(1) == 0)
    def _():
        vbuf_ref[0:SUBLANES, :] = jnp.zeros((SUBLANES, D_MODEL), F32)

    x = x_ref[...]
    xb = x.astype(BF16)
    gb = jnp.dot(xb, win_ref[:, 0:D_MODEL], preferred_element_type=F32)
    gc = jnp.dot(xb, win_ref[:, D_MODEL:2 * D_MODEL], preferred_element_type=F32)
    u = jnp.dot(xb, win_ref[:, 2 * D_MODEL:3 * D_MODEL], preferred_element_type=F32)
    v = gc * u
    vbuf_ref[SUBLANES:SUBLANES + tm, :] = v
    cw = convw_ref[...]
    conv = cw[SC_WIDTH - 1:SC_WIDTH, :] * v
    for k in range(SC_WIDTH - 1):
        shift = SC_WIDTH - 1 - k
        conv = conv + cw[k:k + 1, :] * vbuf_ref[SUBLANES - shift:SUBLANES - shift + tm, :]
    vbuf_ref[0:SUBLANES, :] = vbuf_ref[tm:tm + SUBLANES, :]
    y = (gb * conv).astype(BF16)
    hproj = jnp.dot(y, wout_ref[...], preferred_element_type=F32)
    o_ref[...] = _layernorm(DEEPNORM_ALPHA * x + hproj, lng_ref[...], lnb_ref[...])


def _sconv_mixer(x2d, bsz, seq, w_in, conv_w, w_out, ln_g, ln_b):
    t = x2d.shape[0]
    ns = seq // CONV_TM
    assert seq % CONV_TM == 0
    blk = pl.BlockSpec((CONV_TM, D_MODEL), lambda b, s: (b * ns + s, 0))
    return pl.pallas_call(
        _sconv_kernel,
        out_shape=jax.ShapeDtypeStruct((t, D_MODEL), F32),
        grid=(bsz, ns),
        in_specs=[blk, _resident((D_MODEL, 3 * D_MODEL)), _resident((SC_WIDTH, D_MODEL)),
                  _resident((D_MODEL, D_MODEL)), _resident((1, D_MODEL)), _resident((1, D_MODEL))],
        out_specs=blk,
        scratch_shapes=[pltpu.VMEM((CONV_TM + SUBLANES, D_MODEL), F32)],
        compiler_params=_cparams(2), name="sconv_mixer",
    )(x2d, w_in.astype(BF16), conv_w, w_out.astype(BF16), ln_g.reshape(1, -1), ln_b.reshape(1, -1))


ATT_TM = 512
QKV_COLS = 3 * ATT_COLS


def _stream_view(arr, dil):
    t, c = arr.shape
    span = ATT_BLOCK * dil
    per_tile = ATT_TM // dil
    tiles_per_span = span // ATT_TM
    view = (t // span, dil, ATT_BLOCK, c)
    spec = pl.BlockSpec((None, dil, per_tile, c),
                        lambda i: (i // tiles_per_span, 0, i % tiles_per_span, 0))
    return view, spec


def _deinterleave_rows(slab_ref, dil):
    n_i = ATT_TM // dil
    rows = [jnp.concatenate([slab_ref[s, pl.ds(r, n_i, stride=dil), :] for s in range(ROW_TILE)],
                            axis=1) for r in range(dil)]
    return jnp.concatenate(rows, axis=0).astype(BF16)


def _interleave_rows(slab_ref, blk_ref, dil):
    n_i, c = blk_ref.shape[1], blk_ref.shape[2]
    for r in range(dil):
        br = blk_ref[r].astype(F32)
        for s in range(c // LANES):
            slab_ref[s, pl.ds(r, n_i, stride=dil), :] = br[:, s * LANES:(s + 1) * LANES]
    return jnp.concatenate([slab_ref[s] for s in range(c // LANES)], axis=1)


def _qkv_proj_kernel(x_ref, w_ref, o0_ref, o1_ref, o2_ref, slab_ref):
    x = x_ref[...]
    for s in range(ROW_TILE):
        slab_ref[s] = x[:, s * LANES:(s + 1) * LANES]
    outs = (o0_ref, o1_ref, o2_ref)
    for gi, (_, dil) in enumerate(ATT_GROUPS):
        xb = x.astype(BF16) if dil == 1 else _deinterleave_rows(slab_ref, dil)
        for c in range(QKV_COLS // COL_CHUNK):
            col = gi * QKV_COLS + c * COL_CHUNK
            r = jnp.dot(xb, w_ref[:, col:col + COL_CHUNK], preferred_element_type=F32)
            dst = slice(c * COL_CHUNK, (c + 1) * COL_CHUNK)
            if dil == 1:
                outs[gi][:, dst] = r.astype(BF16)
            else:
                outs[gi][:, :, dst] = r.reshape(dil, ATT_TM // dil, COL_CHUNK).astype(BF16)


def _attn_kernel(q_ref, kp_ref, kc_ref, vp_ref, vc_ref, o_ref, lse_ref, *, dil):
    nq = ATT_BLOCK
    j = pl.program_id(1)
    q = q_ref[...] * (ATT_HEAD_DIM ** -0.5)
    kk = jnp.concatenate([kp_ref[...], kc_ref[...]], axis=0)
    vv = jnp.concatenate([vp_ref[...], vc_ref[...]], axis=0)
    qi = lax.broadcasted_iota(jnp.int32, (nq, 2 * nq), 0)
    kj = lax.broadcasted_iota(jnp.int32, (nq, 2 * nq), 1)
    dist = qi + nq - kj
    first_key = jnp.where(j >= dil, 0, nq)
    valid = (dist >= 0) & (dist <= nq) & (kj >= first_key)
    pen = jnp.where(valid, (dist * dil).astype(F32), jnp.inf)
    lane = lax.broadcasted_iota(jnp.int32, (1, LANES), 1)
    low_half = lane < ATT_HEAD_DIM
    lane_q = lax.broadcasted_iota(jnp.int32, (nq, LANES), 1)
    lse_all = jnp.zeros((nq, LANES), F32)
    zero_b = jnp.zeros((), BF16)
    for pair in range(ATT_HEADS // 2):
        sl = slice(pair * LANES, (pair + 1) * LANES)
        q_pair, k_pair, v_pair = q[:, sl], kk[:, sl], vv[:, sl]
        outs = []
        for half in range(2):
            h = 2 * pair + half
            slope = 2.0 ** (-8.0 * (h + 1) / ATT_HEADS)
            qm = jnp.where(low_half if half == 0 else jnp.logical_not(low_half), q_pair, zero_b)
            s = lax.dot_general(qm, k_pair, (((1,), (1,)), ((), ())), preferred_element_type=F32)
            s = s - slope * pen
            m = jnp.max(s, axis=-1, keepdims=True)
            p = jnp.exp(s - m)
            denom = jnp.sum(p, axis=-1, keepdims=True)
            pv = jnp.dot(p.astype(BF16), v_pair, preferred_element_type=F32)
            outs.append(pv * (1.0 / denom))
            lse_all = jnp.where(lane_q == h, m + jnp.log(denom), lse_all)
        o_ref[:, sl] = jnp.where(low_half, outs[0], outs[1]).astype(o_ref.dtype)
    lse_ref[...] = lse_all


def _attn_group(qkv_g, bsz, seq, gi, dil):
    t = qkv_g.shape[0]
    assert seq % (dil * ATT_BLOCK) == 0
    nblk = seq // ATT_BLOCK
    blk = (ATT_BLOCK, ATT_COLS)
    cur = lambda c: pl.BlockSpec(blk, lambda b, j: (b * nblk + j, c))
    prev = lambda c: pl.BlockSpec(blk, lambda b, j: (b * nblk + jnp.maximum(j - dil, 0), c))
    return pl.pallas_call(
        functools.partial(_attn_kernel, dil=dil),
        out_shape=[jax.ShapeDtypeStruct((t, ATT_COLS), BF16),
                   jax.ShapeDtypeStruct((t, LANES), F32)],
        grid=(bsz, nblk),
        in_specs=[cur(0), prev(1), cur(1), prev(2), cur(2)],
        out_specs=[pl.BlockSpec(blk, lambda b, j: (b * nblk + j, 0)),
                   pl.BlockSpec((ATT_BLOCK, LANES), lambda b, j: (b * nblk + j, 0))],
        compiler_params=_cparams(2), name=f"dil_attn_g{gi}",
    )(qkv_g, qkv_g, qkv_g, qkv_g, qkv_g)


def _attn_merge_kernel(o0_ref, o1_ref, o2_ref, l0_ref, l1_ref, l2_ref, x_ref, e_ref, wout_ref,
                       lng_ref, lnb_ref, out_ref, oslab_ref, lslab_ref):
    dils = [d for _, d in ATT_GROUPS]
    outs = [o0_ref[...].astype(F32)] + [_interleave_rows(oslab_ref, r, d)
                                        for r, d in ((o1_ref, dils[1]), (o2_ref, dils[2]))]
    l0 = l0_ref[...]
    l1 = _interleave_rows(lslab_ref, l1_ref, dils[1])
    l2 = _interleave_rows(lslab_ref, l2_ref, dils[2])
    m = jnp.maximum(jnp.maximum(l0, l1), l2)
    es = [jnp.exp(l0 - m), jnp.exp(l1 - m), jnp.exp(l2 - m)]
    den = es[0] + es[1] + es[2]
    e_mat = e_ref[...]
    acc = None
    for e, o in zip(es, outs):
        w_x = jnp.dot(_split_hi_lo(e / den), e_mat, preferred_element_type=F32)
        acc = w_x * o if acc is None else acc + w_x * o
    hproj = jnp.dot(acc.astype(BF16), wout_ref[...], preferred_element_type=F32)
    out_ref[...] = _layernorm(DEEPNORM_ALPHA * x_ref[...] + hproj, lng_ref[...], lnb_ref[...])


def _attn_mixer(x2d, bsz, seq, w_qkv, w_out, ln_g, ln_b):
    t = x2d.shape[0]
    n_groups = len(ATT_GROUPS)
    assert n_groups == 3 and ATT_GROUPS[0][1] == 1
    for window, dil in ATT_GROUPS:
        assert window // dil == ATT_BLOCK and seq % (ATT_BLOCK * dil) == 0
        assert ATT_TM % dil == 0 and (ATT_BLOCK * dil) % ATT_TM == 0 or dil == 1
    row = lambda w: pl.BlockSpec((ATT_TM, w), lambda i: (i, 0))

    views, specs = zip(*[_stream_view(jax.ShapeDtypeStruct((t, QKV_COLS), BF16), d)
                         for _, d in ATT_GROUPS[1:]])
    qkvs = pl.pallas_call(
        _qkv_proj_kernel,
        out_shape=[jax.ShapeDtypeStruct((t, QKV_COLS), BF16)]
        + [jax.ShapeDtypeStruct(v, BF16) for v in views],
        grid=(t // ATT_TM,),
        in_specs=[row(D_MODEL), _resident((D_MODEL, n_groups * QKV_COLS))],
        out_specs=[row(QKV_COLS)] + list(specs),
        scratch_shapes=[pltpu.VMEM((ROW_TILE, ATT_TM, LANES), F32)],
        compiler_params=_cparams(1), name="att_qkv_proj",
    )(x2d, w_qkv.astype(BF16))

    outs, lses = [], []
    for gi, (_, dil) in enumerate(ATT_GROUPS):
        o, l = _attn_group(qkvs[gi].reshape(t, QKV_COLS), bsz, seq, gi, dil)
        outs.append(o)
        lses.append(l)

    in_arrays, in_specs = [], []
    for group in (outs, lses):
        for arr, (_, dil) in zip(group, ATT_GROUPS):
            if dil == 1:
                in_arrays.append(arr)
                in_specs.append(row(arr.shape[1]))
            else:
                view, spec = _stream_view(arr, dil)
                in_arrays.append(arr.reshape(view))
                in_specs.append(spec)
    return pl.pallas_call(
        _attn_merge_kernel,
        out_shape=jax.ShapeDtypeStruct((t, D_MODEL), F32),
        grid=(t // ATT_TM,),
        in_specs=in_specs + [row(D_MODEL), _resident((2 * LANES, ATT_COLS)),
                             _resident((ATT_COLS, D_MODEL)),
                             _resident((1, D_MODEL)), _resident((1, D_MODEL))],
        out_specs=row(D_MODEL),
        scratch_shapes=[pltpu.VMEM((ATT_COLS // LANES, ATT_TM, LANES), F32),
                        pltpu.VMEM((1, ATT_TM, LANES), F32)],
        compiler_params=_cparams(1), name="att_merge",
    )(*in_arrays, x2d, _expansion_matrix(ATT_HEADS, ATT_HEAD_DIM), w_out.astype(BF16),
      ln_g.reshape(1, -1), ln_b.reshape(1, -1))


ROUTE_W1, ROUTE_W2, ROUTE_E1, ROUTE_E2, ROUTE_S1, ROUTE_S2 = range(6)
EXPERT_LANE0 = MOE_GROUPS
MOE_TB = 512
RUN_ALIGN = 16
LOCAL_SLOTS = 1536
assert LOCAL_SLOTS >= MOE_TOP_K * MOE_TB + MOE_N_EXPERTS * (RUN_ALIGN - 1)
assert ROUTE_TM == MOE_TB
MAX_PIECES = LOCAL_SLOTS // RUN_ALIGN


def _n_slot_tiles(t):
    n_blocks = t // MOE_TB
    worst = t * MOE_TOP_K + n_blocks * MOE_N_EXPERTS * (RUN_ALIGN - 1) + MOE_N_EXPERTS * (FFN_TM - 1)
    return -(-worst // FFN_TM), (t * MOE_TOP_K) // FFN_TM


def _router_kernel(x_ref, wr_ref, br_ref, route_ref, cnt_ref):
    tm = ROUTE_TM
    logits = _dot_3pass(x_ref[...], wr_ref[...]) + br_ref[...]
    lane_i = lax.broadcasted_iota(jnp.int32, (tm, LANES), 1)
    lane = lane_i.astype(F32)
    big = float(LANES)

    def first_argmax(vals):
        vmax = jnp.max(vals, axis=-1, keepdims=True)
        idx = jnp.min(jnp.where(vals == vmax, lane, big), axis=-1, keepdims=True)
        return vmax, idx

    gl = jnp.where(lane_i < MOE_GROUPS, logits, -jnp.inf)
    gmax, gsel = first_argmax(gl)
    gw = 1.0 / jnp.sum(jnp.exp(gl - gmax), axis=-1, keepdims=True)
    grp_of_lane = jnp.right_shift(lane_i - EXPERT_LANE0, 3).astype(F32)
    in_grp = ((lane_i >= EXPERT_LANE0) & (lane_i < EXPERT_LANE0 + MOE_N_EXPERTS)
              & (grp_of_lane == gsel))
    el = jnp.where(in_grp, logits, -jnp.inf)
    v1, i1 = first_argmax(el)
    v2, i2 = first_argmax(jnp.where(lane == i1, -jnp.inf, el))
    e2 = jnp.exp(v2 - v1)
    w1 = gw / (1.0 + e2)
    w2 = gw * e2 / (1.0 + e2)

    sel1, sel2 = lane == i1, lane == i2
    onehot = jnp.where(sel1 | sel2, 1.0, 0.0)
    r_i = lax.broadcasted_iota(jnp.int32, (tm, tm), 0)
    c_i = lax.broadcasted_iota(jnp.int32, (tm, tm), 1)
    strict = jnp.where(c_i < r_i, 1.0, 0.0).astype(BF16)
    before = jnp.dot(strict, onehot.astype(BF16), preferred_element_type=F32)
    cnt = jnp.sum(onehot, axis=0, keepdims=True)
    run_tiles = jnp.floor((cnt + (RUN_ALIGN - 1)) * (1.0 / RUN_ALIGN))
    e_r = lax.broadcasted_iota(jnp.int32, (LANES, LANES), 0)
    e_c = lax.broadcasted_iota(jnp.int32, (LANES, LANES), 1)
    earlier = jnp.where(e_r < e_c, 1.0, 0.0).astype(BF16)
    run_start = RUN_ALIGN * jnp.dot(jnp.broadcast_to(run_tiles, (SUBLANES, LANES)).astype(BF16),
                                    earlier, preferred_element_type=F32)[0:1, :]
    slot = before + run_start
    slot1 = jnp.sum(jnp.where(sel1, slot, 0.0), axis=-1, keepdims=True)
    slot2 = jnp.sum(jnp.where(sel2, slot, 0.0), axis=-1, keepdims=True)
    cnt_ref[...] = jnp.broadcast_to(cnt, (SUBLANES, LANES))

    out = jnp.zeros((tm, LANES), F32)
    for pos, val in ((ROUTE_W1, w1), (ROUTE_W2, w2), (ROUTE_E1, i1 - EXPERT_LANE0),
                     (ROUTE_E2, i2 - EXPERT_LANE0), (ROUTE_S1, slot1), (ROUTE_S2, slot2)):
        out = jnp.where(lane_i == pos, val, out)
    route_ref[...] = out


def _start_run_copies(tabs, blk, make_copy):
    grow_ref, ntot_ref = tabs
    buf = blk % 2

    def per_piece(j, c):
        g = grow_ref[blk * MAX_PIECES + j]
        make_copy(buf, pl.multiple_of(j * RUN_ALIGN, RUN_ALIGN), pl.multiple_of(g, RUN_ALIGN)).start()
        return c

    lax.fori_loop(0, ntot_ref[blk], per_piece, 0)


def _wait_run_copies(tabs, blk, make_copy):
    buf = blk % 2

    def wait_one(j, c):
        make_copy(buf, 0, 0).wait()
        return c

    lax.fori_loop(0, tabs[1][blk], wait_one, 0)


def _dispatch_kernel(grow_ref, ntot_ref, zrow_ref, zflag_ref,
                     x_ref, route_ref, xs_hbm, loc_ref, zero_ref, sem, zsem):
    i = pl.program_id(0)
    n_zero = zrow_ref.shape[0]

    @pl.when(i == 0)
    def _():
        zero_ref[...] = jnp.zeros_like(zero_ref)

        def zcopy(e):
            start = pl.multiple_of(zrow_ref[e], FFN_TM)
            return pltpu.make_async_copy(zero_ref, xs_hbm.at[pl.ds(start, FFN_TM), :], zsem)

        def zstart(e, carry):
            @pl.when(zflag_ref[e] > 0)
            def _():
                zcopy(e).start()
            return carry

        def zwait(e, carry):
            @pl.when(zflag_ref[e] > 0)
            def _():
                zcopy(e).wait()
            return carry

        lax.fori_loop(0, n_zero, zstart, 0)
        lax.fori_loop(0, n_zero, zwait, 0)

    slots_t = route_ref[...].T
    s1 = slots_t[ROUTE_S1:ROUTE_S1 + 1, :]
    s2 = slots_t[ROUTE_S2:ROUTE_S2 + 1, :]
    xb = x_ref[...].astype(BF16)
    piece = MOE_TB
    buf = i % 2
    for p in range(LOCAL_SLOTS // piece):
        srow = (lax.broadcasted_iota(jnp.int32, (piece, MOE_TB), 0) + p * piece).astype(F32)
        perm = jnp.where((srow == s1) | (srow == s2), 1.0, 0.0).astype(BF16)
        loc_ref[buf, p * piece:(p + 1) * piece, :] = jnp.dot(
            perm, xb, preferred_element_type=F32).astype(BF16)

    def make_copy(b, l, g):
        return pltpu.make_async_copy(loc_ref.at[b, pl.ds(l, RUN_ALIGN), :],
                                     xs_hbm.at[pl.ds(g, RUN_ALIGN), :], sem.at[b])

    tabs = (grow_ref, ntot_ref)
    _start_run_copies(tabs, i, make_copy)

    @pl.when(i > 0)
    def _():
        _wait_run_copies(tabs, i - 1, make_copy)

    @pl.when(i == pl.num_programs(0) - 1)
    def _():
        _wait_run_copies(tabs, i, make_copy)


def _ffn_kernel(te_ref, nused_ref, xs_ref, wg_ref, wu_ref, wd_ref, ys_ref, wgb_ref, wub_ref, wdb_ref):
    i = pl.program_id(0)

    @pl.when((i == 0) | (te_ref[i] != te_ref[jnp.maximum(i - 1, 0)]))
    def _():
        wgb_ref[...] = wg_ref[...].astype(BF16)
        wub_ref[...] = wu_ref[...].astype(BF16)
        wdb_ref[...] = wd_ref[...].astype(BF16)

    @pl.when(i < nused_ref[0])
    def _():
        x = xs_ref[...]
        g = jnp.dot(x, wgb_ref[...], preferred_element_type=F32)
        u = jnp.dot(x, wub_ref[...], preferred_element_type=F32)
        hid = (_silu(g) * u).astype(BF16)
        ys_ref[...] = jnp.dot(hid, wdb_ref[...], preferred_element_type=F32).astype(ys_ref.dtype)

    @pl.when(i >= nused_ref[0])
    def _():
        ys_ref[...] = jnp.zeros_like(ys_ref)


def _combine_kernel(grow_ref, ntot_ref, ys_hbm, route_ref, x_ref,
                    lng_ref, lnb_ref, o_ref, loc_ref, sem):
    i = pl.program_id(0)

    def make_copy(b, l, g):
        return pltpu.make_async_copy(ys_hbm.at[pl.ds(g, RUN_ALIGN), :],
                                     loc_ref.at[b, pl.ds(l, RUN_ALIGN), :], sem.at[b])

    tabs = (grow_ref, ntot_ref)

    @pl.when(i == 0)
    def _():
        loc_ref[...] = jnp.zeros_like(loc_ref)
        _start_run_copies(tabs, i, make_copy)

    @pl.when(i + 1 < pl.num_programs(0))
    def _():
        _start_run_copies(tabs, i + 1, make_copy)

    _wait_run_copies(tabs, i, make_copy)

    route = route_ref[...]
    w1, w2 = route[:, ROUTE_W1:ROUTE_W1 + 1], route[:, ROUTE_W2:ROUTE_W2 + 1]
    s1, s2 = route[:, ROUTE_S1:ROUTE_S1 + 1], route[:, ROUTE_S2:ROUTE_S2 + 1]
    piece = MOE_TB
    buf = i % 2
    f = None
    for p in range(LOCAL_SLOTS // piece):
        scol = (lax.broadcasted_iota(jnp.int32, (MOE_TB, piece), 1) + p * piece).astype(F32)
        comb = (jnp.where(scol == s1, w1, 0.0) + jnp.where(scol == s2, w2, 0.0)).astype(BF16)
        part = jnp.dot(comb, loc_ref[buf, p * piece:(p + 1) * piece, :], preferred_element_type=F32)
        f = part if f is None else f + part
    o_ref[...] = _layernorm(DEEPNORM_ALPHA * x_ref[...] + f, lng_ref[...], lnb_ref[...])


def _moe_layer(x2d, layer, wg, bg, we, be, w_gate, w_up, w_down, ln_g, ln_b):
    t = x2d.shape[0]
    assert t % MOE_TB == 0
    n_blocks = t // MOE_TB
    n_tiles, min_tiles = _n_slot_tiles(t)
    n_slots = n_tiles * FFN_TM

    pad_cols = LANES - MOE_GROUPS - MOE_N_EXPERTS
    wr = jnp.pad(jnp.concatenate([wg, we], axis=1), ((0, 0), (0, pad_cols)))
    br = jnp.pad(jnp.concatenate([bg, be]), (0, pad_cols)).reshape(1, LANES)
    route, counts = pl.pallas_call(
        _router_kernel,
        out_shape=[jax.ShapeDtypeStruct((t, LANES), F32),
                   jax.ShapeDtypeStruct((n_blocks * SUBLANES, LANES), F32)],
        grid=(n_blocks,),
        in_specs=[pl.BlockSpec((ROUTE_TM, D_MODEL), lambda i: (i, 0)),
                  _resident((D_MODEL, LANES)), _resident((1, LANES))],
        out_specs=[pl.BlockSpec((ROUTE_TM, LANES), lambda i: (i, 0)),
                   pl.BlockSpec((SUBLANES, LANES), lambda i: (i, 0))],
        compiler_params=_cparams(1), name="moe_router",
    )(x2d, wr, br)

    cnt = counts.reshape(n_blocks, SUBLANES, LANES)[:, 0, EXPERT_LANE0:EXPERT_LANE0 + MOE_N_EXPERTS]
    cnt = cnt.astype(jnp.int32)
    run = ((cnt + RUN_ALIGN - 1) // RUN_ALIGN) * RUN_ALIGN
    lstart = jnp.cumsum(run, axis=1) - run
    seg_len = jnp.sum(run, axis=0)
    seg_pad = ((seg_len + FFN_TM - 1) // FFN_TM) * FFN_TM
    ends = jnp.cumsum(seg_pad)
    gstart = (ends - seg_pad)[None, :] + jnp.cumsum(run, axis=0) - run
    piece_row = jnp.arange(MAX_PIECES, dtype=jnp.int32)[None, :, None] * RUN_ALIGN
    owner = jnp.sum(((lstart + run)[:, None, :] <= piece_row).astype(jnp.int32), axis=2)
    owner = jnp.minimum(owner, MOE_N_EXPERTS - 1)
    grow = (jnp.take_along_axis(gstart - lstart, owner, axis=1) + piece_row[:, :, 0])
    tabs = [grow.astype(jnp.int32).reshape(-1),
            (jnp.sum(run, axis=1) // RUN_ALIGN).astype(jnp.int32)]
    tile_start = jnp.arange(n_tiles, dtype=jnp.int32) * FFN_TM
    tile_expert = jnp.minimum(jnp.sum(ends[None, :] <= tile_start[:, None], axis=1),
                              MOE_N_EXPERTS - 1).astype(jnp.int32)
    n_used = (ends[-1] // FFN_TM).astype(jnp.int32).reshape(1)
    tail_tile = jnp.arange(min_tiles, n_tiles, dtype=jnp.int32)
    zrow = jnp.concatenate([jnp.maximum(ends - FFN_TM, 0), tail_tile * FFN_TM]).astype(jnp.int32)
    zflag = jnp.concatenate([seg_pad > 0, tail_tile >= n_used[0]]).astype(jnp.int32)

    blk_row = lambda w: pl.BlockSpec((MOE_TB, w), lambda i, *_: (i, 0))
    xs = pl.pallas_call(
        _dispatch_kernel,
        out_shape=jax.ShapeDtypeStruct((n_slots, D_MODEL), BF16),
        grid_spec=pltpu.PrefetchScalarGridSpec(
            num_scalar_prefetch=4, grid=(n_blocks,),
            in_specs=[blk_row(D_MODEL), blk_row(LANES)],
            out_specs=pl.BlockSpec(memory_space=pl.ANY),
            scratch_shapes=[pltpu.VMEM((2, LOCAL_SLOTS, D_MODEL), BF16),
                            pltpu.VMEM((FFN_TM, D_MODEL), BF16),
                            pltpu.SemaphoreType.DMA((2,)), pltpu.SemaphoreType.DMA]),
        compiler_params=_cparams(1), name="moe_dispatch",
    )(*tabs, zrow, zflag, x2d, route)

    w_spec = lambda r, c: pl.BlockSpec((None, None, r, c), lambda i, te, nu: (layer, te[i], 0, 0))
    slot_blk = pl.BlockSpec((FFN_TM, D_MODEL), lambda i, te, nu: (i, 0))
    used_blk = pl.BlockSpec((FFN_TM, D_MODEL), lambda i, te, nu: (jnp.minimum(i, nu[0] - 1), 0))
    ys = pl.pallas_call(
        _ffn_kernel,
        out_shape=jax.ShapeDtypeStruct((n_slots, D_MODEL), BF16),
        grid_spec=pltpu.PrefetchScalarGridSpec(
            num_scalar_prefetch=2, grid=(n_tiles,),
            in_specs=[used_blk, w_spec(D_MODEL, MOE_D_EXPERT), w_spec(D_MODEL, MOE_D_EXPERT),
                      w_spec(MOE_D_EXPERT, D_MODEL)],
            out_specs=slot_blk,
            scratch_shapes=[pltpu.VMEM((D_MODEL, MOE_D_EXPERT), BF16),
                            pltpu.VMEM((D_MODEL, MOE_D_EXPERT), BF16),
                            pltpu.VMEM((MOE_D_EXPERT, D_MODEL), BF16)]),
        compiler_params=_cparams(1), name="moe_ffn",
    )(tile_expert, n_used, xs, w_gate, w_up, w_down)

    vec = pl.BlockSpec((1, D_MODEL), lambda i, *_: (0, 0))
    return pl.pallas_call(
        _combine_kernel,
        out_shape=jax.ShapeDtypeStruct((t, D_MODEL), F32),
        grid_spec=pltpu.PrefetchScalarGridSpec(
            num_scalar_prefetch=2, grid=(n_blocks,),
            in_specs=[pl.BlockSpec(memory_space=pl.ANY), blk_row(LANES), blk_row(D_MODEL), vec, vec],
            out_specs=blk_row(D_MODEL),
            scratch_shapes=[pltpu.VMEM((2, LOCAL_SLOTS, D_MODEL), BF16),
                            pltpu.SemaphoreType.DMA((2,))]),
        compiler_params=_cparams(1), name="moe_combine",
    )(*tabs, ys, route, x2d, ln_g.reshape(1, -1), ln_b.reshape(1, -1))


def kernel(x, ssd_w_in, ssd_conv_w, ssd_conv_b, ssd_dt_bias, ssd_a_log, ssd_d, ssd_norm_w, ssd_w_out,
           sc_w_in, sc_conv_w, sc_w_out, att_w_qkv, att_w_out,
           moe_wg, moe_bg, moe_we, moe_be, moe_w_gate, moe_w_up, moe_w_down, ln_g, ln_b):
    bsz, seq, d = x.shape
    assert d == D_MODEL
    h = x.reshape(bsz * seq, d)
    for i in range(DEPTH):
        kind, j = i % N_MIXERS, i // N_MIXERS
        if kind == 0:
            h = _ssd_mixer(h, bsz, seq, ssd_w_in[j], ssd_conv_w[j], ssd_conv_b[j], ssd_dt_bias[j],
                           ssd_a_log[j], ssd_d[j], ssd_norm_w[j], ssd_w_out[j], ln_g[i, 0], ln_b[i, 0])
        elif kind == 1:
            h = _sconv_mixer(h, bsz, seq, sc_w_in[j], sc_conv_w[j], sc_w_out[j], ln_g[i, 0], ln_b[i, 0])
        else:
            h = _attn_mixer(h, bsz, seq, att_w_qkv[j], att_w_out[j], ln_g[i, 0], ln_b[i, 0])
        h = _moe_layer(h, i, moe_wg[i], moe_bg[i], moe_we[i], moe_be[i],
                       moe_w_gate, moe_w_up, moe_w_down, ln_g[i, 1], ln_b[i, 1])
    return h.reshape(bsz, seq, d)
```

```python
import functools

import numpy as np
import jax
import jax.numpy as jnp
from jax import lax
from jax.experimental import pallas as pl
from jax.experimental.pallas import tpu as pltpu

D_MODEL = 1024
DEPTH = 4
N_MIXERS = 3
DEEPNORM_ALPHA = (2 * DEPTH) ** 0.25
LN_EPS = 1e-5

SSD_D_INNER = 2 * D_MODEL
SSD_HEAD_DIM = 64
SSD_N_HEADS = SSD_D_INNER // SSD_HEAD_DIM
SSD_N_GROUPS = 4
SSD_D_STATE = 128
SSD_CONV_WIDTH = 4
SSD_CHUNK = 256
SSD_SUBCHUNK = 128
SSD_CONV_DIM = SSD_D_INNER + 2 * SSD_N_GROUPS * SSD_D_STATE
SSD_HEADS_PER_GROUP = SSD_N_HEADS // SSD_N_GROUPS
SSD_GROUP_WIDTH = SSD_D_INNER // SSD_N_GROUPS

SC_WIDTH = 3

ATT_GROUPS = ((128, 1), (512, 4), (2048, 16))
ATT_HEAD_DIM = 64
ATT_HEADS = D_MODEL // ATT_HEAD_DIM
ATT_BLOCK = 128
ATT_COLS = ATT_HEADS * ATT_HEAD_DIM

MOE_GROUPS = 4
MOE_EXPERTS_PER_GROUP = 8
MOE_N_EXPERTS = MOE_GROUPS * MOE_EXPERTS_PER_GROUP
MOE_TOP_K = 2
MOE_D_EXPERT = 256

LANES = 128
SUBLANES = 8
VMEM_LIMIT_BYTES = 56 * 1024 * 1024

PROJ_TM = 512
CONV_TM = 512
MERGE_TM = 512
ROUTE_TM = 512
FFN_TM = 512
COL_CHUNK = 512
ROW_TILE = D_MODEL // LANES

F32 = jnp.float32
BF16 = jnp.bfloat16
HIGHEST = lax.Precision.HIGHEST


def _cparams(n_axes):
    return pltpu.CompilerParams(dimension_semantics=("arbitrary",) * n_axes,
                                vmem_limit_bytes=VMEM_LIMIT_BYTES)


def _resident(shape):
    nd = len(shape)
    return pl.BlockSpec(shape, lambda *_: (0,) * nd, pipeline_mode=pl.Buffered(1))


def _silu(v):
    return v * (1.0 / (1.0 + jnp.exp(-v)))


def _softplus(v):
    return jnp.maximum(v, 0.0) + jnp.log1p(jnp.exp(-jnp.abs(v)))


def _layernorm(v, g, b):
    mu = jnp.mean(v, axis=-1, keepdims=True)
    d = v - mu
    var = jnp.mean(d * d, axis=-1, keepdims=True)
    return d * lax.rsqrt(var + LN_EPS) * g + b


def _split_hi_lo(v):
    hi = v.astype(BF16)
    lo = (v - hi.astype(F32)).astype(BF16)
    return jnp.concatenate([hi, lo], axis=1)


def _expansion_matrix(n_heads, width):
    e = np.zeros((2 * LANES, n_heads * width), np.float32)
    for h in range(n_heads):
        e[h, h * width:(h + 1) * width] = 1.0
        e[LANES + h, h * width:(h + 1) * width] = 1.0
    return jnp.asarray(e, dtype=BF16)


def _dot_3pass(x, w):
    xh = x.astype(BF16)
    xl = (x - xh.astype(F32)).astype(BF16)
    wh = w.astype(BF16)
    wl = (w - wh.astype(F32)).astype(BF16)
    return (jnp.dot(xh, wh, preferred_element_type=F32) + jnp.dot(xl, wh, preferred_element_type=F32)
            + jnp.dot(xh, wl, preferred_element_type=F32))


def _proj_kernel(x_ref, w_ref, wdt_ref, *outs, widths):
    x = x_ref[...]
    xb = x.astype(BF16)
    off = 0
    for j, n in enumerate(widths):
        for c in range(n // COL_CHUNK):
            r = jnp.dot(xb, w_ref[:, off + c * COL_CHUNK: off + (c + 1) * COL_CHUNK],
                        preferred_element_type=F32)
            outs[j][:, c * COL_CHUNK:(c + 1) * COL_CHUNK] = r.astype(outs[j].dtype)
        off += n
    outs[len(widths)][...] = _dot_3pass(x, wdt_ref[...])


def _project(x, w_bf16, widths, w_dt, name):
    t, k = x.shape
    n_total = sum(widths)
    assert w_bf16.shape == (k, n_total) and t % PROJ_TM == 0
    row = lambda n: pl.BlockSpec((PROJ_TM, n), lambda i: (i, 0))
    return pl.pallas_call(
        functools.partial(_proj_kernel, widths=tuple(widths)),
        out_shape=[jax.ShapeDtypeStruct((t, n), BF16) for n in widths]
        + [jax.ShapeDtypeStruct((t, w_dt.shape[1]), F32)],
        grid=(t // PROJ_TM,),
        in_specs=[row(k), _resident((k, n_total)), _resident(w_dt.shape)],
        out_specs=[row(n) for n in widths] + [row(w_dt.shape[1])],
        compiler_params=_cparams(1), name=name)(x, w_bf16, w_dt)


def _ssd_kernel(z_ref, xbc_ref, dt_ref, x_ref, convw_ref, convb_ref, dtb_ref, alog_ref,
                dskip_ref, normw_ref, wout_ref, lng_ref, lnb_ref, e_ref,
                o_ref, state_ref, carry_ref, y_ref):
    L = SSD_CHUNK
    SUB = SSD_SUBCHUNK
    N = SSD_D_STATE
    GW = SSD_GROUP_WIDTH
    c_idx = pl.program_id(1)

    @pl.when(c_idx == 0)
    def _():
        state_ref[...] = jnp.zeros_like(state_ref)
        carry_ref[...] = jnp.zeros_like(carry_ref)

    u_b = xbc_ref[...]
    u = u_b.astype(F32)
    cw = convw_ref[...]
    row = lax.broadcasted_iota(jnp.int32, (L, L), 0)
    col = lax.broadcasted_iota(jnp.int32, (L, L), 1)
    carry = carry_ref[...]
    row8 = lax.broadcasted_iota(jnp.int32, (SUBLANES, SSD_CONV_DIM), 0)
    acc = convb_ref[...] + cw[SSD_CONV_WIDTH - 1:SSD_CONV_WIDTH, :] * u
    head = jnp.zeros((SUBLANES, SSD_CONV_DIM), F32)
    for k in range(1, SSD_CONV_WIDTH):
        wk = cw[SSD_CONV_WIDTH - 1 - k:SSD_CONV_WIDTH - k, :]
        shift_mat = jnp.where(row - col == k, 1.0, 0.0).astype(BF16)
        acc = acc + wk * jnp.dot(shift_mat, u_b, preferred_element_type=F32)
        head = head + wk * jnp.where(row8 < k, pltpu.roll(carry, k, 0), 0.0)
    carry_ref[...] = u[L - SUBLANES:, :]
    acc = jnp.concatenate([acc[:SUBLANES] + head, acc[SUBLANES:]], axis=0)
    act = _silu(acc)
    xs = act[:, :SSD_D_INNER]
    bmat = act[:, SSD_D_INNER:SSD_D_INNER + SSD_N_GROUPS * N]
    cmat = act[:, SSD_D_INNER + SSD_N_GROUPS * N:]

    dt = _softplus(dt_ref[...] + dtb_ref[...])
    a = -jnp.exp(alog_ref[...])
    dta = dt * a
    rs = lax.broadcasted_iota(jnp.int32, (SUB, SUB), 0)
    cs_i = lax.broadcasted_iota(jnp.int32, (SUB, SUB), 1)
    causal = rs >= cs_i
    causal_f = causal.astype(F32)
    e_mat = e_ref[...]
    lane = lax.broadcasted_iota(jnp.int32, (1, LANES), 1)
    low_half = lane < SSD_HEAD_DIM
    dskip = dskip_ref[...]

    for sc in range(L // SUB):
        rows = slice(sc * SUB, (sc + 1) * SUB)
        dt_s = dt[rows]
        cs = jnp.dot(causal_f, dta[rows], precision=HIGHEST, preferred_element_type=F32)
        cs_t = cs.T
        last = cs[SUB - 1:SUB, :]
        xs_s = xs[rows]
        ecs_x = jnp.dot(_split_hi_lo(jnp.exp(cs)), e_mat, preferred_element_type=F32)
        ws_x = jnp.dot(_split_hi_lo(jnp.exp(last - cs) * dt_s), e_mat, preferred_element_type=F32)
        dt_x = jnp.dot(_split_hi_lo(dt_s), e_mat, preferred_element_type=F32)
        elast_x = jnp.dot(_split_hi_lo(jnp.broadcast_to(jnp.exp(last), (SUBLANES, LANES))), e_mat,
                          preferred_element_type=F32)[0:1, :]
        xw_b = (xs_s * ws_x).astype(BF16)
        xdt_b = (xs_s * dt_x).astype(BF16)
        for g in range(SSD_N_GROUPS):
            bg = bmat[rows, g * N:(g + 1) * N]
            cg_b = cmat[rows, g * N:(g + 1) * N].astype(BF16)
            cb = lax.dot_general(cg_b, bg.astype(BF16), (((1,), (1,)), ((), ())),
                                 preferred_element_type=F32)
            s_old = state_ref[g]
            yoff = jnp.dot(cg_b, s_old.astype(BF16), preferred_element_type=F32)
            for pr in range(SSD_HEADS_PER_GROUP // 2):
                pair = g * (SSD_HEADS_PER_GROUP // 2) + pr
                cols = slice(pair * LANES, (pair + 1) * LANES)
                x_pair = xdt_b[:, cols]
                ys = []
                for half in range(2):
                    h = 2 * pair + half
                    seg = cs[:, h:h + 1] - cs_t[h:h + 1, :]
                    mix = cb * jnp.exp(jnp.where(causal, seg, -jnp.inf))
                    ys.append(jnp.dot(mix.astype(BF16), x_pair, preferred_element_type=F32))
                off_cols = slice(pr * LANES, (pr + 1) * LANES)
                y_ref[rows, cols] = (jnp.where(low_half, ys[0], ys[1])
                                     + yoff[:, off_cols] * ecs_x[:, cols] + dskip[:, cols] * xs_s[:, cols])
            upd = jnp.dot(bg.T.astype(BF16), xw_b[:, g * GW:(g + 1) * GW], preferred_element_type=F32)
            state_ref[g] = s_old * elast_x[:, g * GW:(g + 1) * GW] + upd

    y = y_ref[...] * _silu(z_ref[...].astype(F32))
    normw = normw_ref[...]
    yn_parts = []
    for g in range(SSD_N_GROUPS):
        yg = y[:, g * GW:(g + 1) * GW]
        ms = jnp.mean(yg * yg, axis=-1, keepdims=True)
        yn_parts.append((yg * lax.rsqrt(ms + LN_EPS) * normw[:, g * GW:(g + 1) * GW]).astype(BF16))
    yn = jnp.concatenate(yn_parts, axis=1)
    hproj = jnp.dot(yn, wout_ref[...], preferred_element_type=F32)
    o_ref[...] = _layernorm(DEEPNORM_ALPHA * x_ref[...] + hproj, lng_ref[...], lnb_ref[...])


def _ssd_mixer(x2d, bsz, seq, w_in, conv_w, conv_b, dt_bias, a_log, d_skip, norm_w, w_out,
               ln_g, ln_b):
    t = x2d.shape[0]
    nc = seq // SSD_CHUNK
    assert seq % SSD_CHUNK == 0
    n_main = SSD_D_INNER + SSD_CONV_DIM
    w_main = w_in[:, :n_main].astype(BF16)
    w_dt = jnp.pad(w_in[:, n_main:], ((0, 0), (0, LANES - SSD_N_HEADS)))
    z, xbc, dt_raw = _project(x2d, w_main, (SSD_D_INNER, SSD_CONV_DIM), w_dt, name="ssd_in_proj")
    pad = (0, LANES - SSD_N_HEADS)
    dtb = jnp.pad(dt_bias, pad).reshape(1, LANES)
    alog = jnp.pad(a_log, pad).reshape(1, LANES)
    dskip = jnp.repeat(d_skip, SSD_HEAD_DIM).reshape(1, SSD_D_INNER)
    L = SSD_CHUNK
    blk = lambda w: pl.BlockSpec((L, w), lambda b, c: (b * nc + c, 0))
    return pl.pallas_call(
        _ssd_kernel,
        out_shape=jax.ShapeDtypeStruct((t, D_MODEL), F32),
        grid=(bsz, nc),
        in_specs=[blk(SSD_D_INNER), blk(SSD_CONV_DIM), blk(LANES), blk(D_MODEL),
                  _resident((SSD_CONV_WIDTH, SSD_CONV_DIM)), _resident((1, SSD_CONV_DIM)),
                  _resident((1, LANES)), _resident((1, LANES)),
                  _resident((1, SSD_D_INNER)), _resident((1, SSD_D_INNER)),
                  _resident((SSD_D_INNER, D_MODEL)),
                  _resident((1, D_MODEL)), _resident((1, D_MODEL)),
                  _resident((2 * LANES, SSD_D_INNER))],
        out_specs=blk(D_MODEL),
        scratch_shapes=[pltpu.VMEM((SSD_N_GROUPS, SSD_D_STATE, SSD_GROUP_WIDTH), F32),
                        pltpu.VMEM((SUBLANES, SSD_CONV_DIM), F32),
                        pltpu.VMEM((L, SSD_D_INNER), F32)],
        compiler_params=_cparams(2), name="ssd_scan",
    )(z, xbc, dt_raw, x2d, conv_w, conv_b.reshape(1, -1), dtb, alog, dskip,
      norm_w.reshape(1, -1), w_out.astype(BF16), ln_g.reshape(1, -1), ln_b.reshape(1, -1),
      _expansion_matrix(SSD_N_HEADS, SSD_HEAD_DIM))


def _sconv_kernel(x_ref, win_ref, convw_ref, wout_ref, lng_ref, lnb_ref, o_ref, vbuf_ref):
    tm = CONV_TM

    @pl.when(pl.program_id(1) == 0)
    def _():
        vbuf_ref[0:SUBLANES, :] = jnp.zeros((SUBLANES, D_MODEL), F32)

    x = x_ref[...]
    xb = x.astype(BF16)
    gb = jnp.dot(xb, win_ref[:, 0:D_MODEL], preferred_element_type=F32)
    gc = jnp.dot(xb, win_ref[:, D_MODEL:2 * D_MODEL], preferred_element_type=F32)
    u = jnp.dot(xb, win_ref[:, 2 * D_MODEL:3 * D_MODEL], preferred_element_type=F32)
    v = gc * u
    vbuf_ref[SUBLANES:SUBLANES + tm, :] = v
    cw = convw_ref[...]
    conv = cw[SC_WIDTH - 1:SC_WIDTH, :] * v
    for k in range(SC_WIDTH - 1):
        shift = SC_WIDTH - 1 - k
        conv = conv + cw[k:k + 1, :] * vbuf_ref[SUBLANES - shift:SUBLANES - shift + tm, :]
    vbuf_ref[0:SUBLANES, :] = vbuf_ref[tm:tm + SUBLANES, :]
    y = (gb * conv).astype(BF16)
    hproj = jnp.dot(y, wout_ref[...], preferred_element_type=F32)
    o_ref[...] = _layernorm(DEEPNORM_ALPHA * x + hproj, lng_ref[...], lnb_ref[...])


def _sconv_mixer(x2d, bsz, seq, w_in, conv_w, w_out, ln_g, ln_b):
    t = x2d.shape[0]
    ns = seq // CONV_TM
    assert seq % CONV_TM == 0
    blk = pl.BlockSpec((CONV_TM, D_MODEL), lambda b, s: (b * ns + s, 0))
    return pl.pallas_call(
        _sconv_kernel,
        out_shape=jax.ShapeDtypeStruct((t, D_MODEL), F32),
        grid=(bsz, ns),
        in_specs=[blk, _resident((D_MODEL, 3 * D_MODEL)), _resident((SC_WIDTH, D_MODEL)),
                  _resident((D_MODEL, D_MODEL)), _resident((1, D_MODEL)), _resident((1, D_MODEL))],
        out_specs=blk,
        scratch_shapes=[pltpu.VMEM((CONV_TM + SUBLANES, D_MODEL), F32)],
        compiler_params=_cparams(2), name="sconv_mixer",
    )(x2d, w_in.astype(BF16), conv_w, w_out.astype(BF16), ln_g.reshape(1, -1), ln_b.reshape(1, -1))


ATT_TM = 512
QKV_COLS = 3 * ATT_COLS


def _stream_view(arr, dil):
    t, c = arr.shape
    span = ATT_BLOCK * dil
    per_tile = ATT_TM // dil
    tiles_per_span = span // ATT_TM
    view = (t // span, dil, ATT_BLOCK, c)
    spec = pl.BlockSpec((None, dil, per_tile, c),
                        lambda i: (i // tiles_per_span, 0, i % tiles_per_span, 0))
    return view, spec


def _deinterleave_rows(slab_ref, dil):
    n_i = ATT_TM // dil
    rows = [jnp.concatenate([slab_ref[s, pl.ds(r, n_i, stride=dil), :] for s in range(ROW_TILE)],
                            axis=1) for r in range(dil)]
    return jnp.concatenate(rows, axis=0).astype(BF16)


def _interleave_rows(slab_ref, blk_ref, dil):
    n_i, c = blk_ref.shape[1], blk_ref.shape[2]
    for r in range(dil):
        br = blk_ref[r].astype(F32)
        for s in range(c // LANES):
            slab_ref[s, pl.ds(r, n_i, stride=dil), :] = br[:, s * LANES:(s + 1) * LANES]
    return jnp.concatenate([slab_ref[s] for s in range(c // LANES)], axis=1)


def _qkv_proj_kernel(x_ref, w_ref, o0_ref, o1_ref, o2_ref, slab_ref):
    x = x_ref[...]
    for s in range(ROW_TILE):
        slab_ref[s] = x[:, s * LANES:(s + 1) * LANES]
    outs = (o0_ref, o1_ref, o2_ref)
    for gi, (_, dil) in enumerate(ATT_GROUPS):
        xb = x.astype(BF16) if dil == 1 else _deinterleave_rows(slab_ref, dil)
        for c in range(QKV_COLS // COL_CHUNK):
            col = gi * QKV_COLS + c * COL_CHUNK
            r = jnp.dot(xb, w_ref[:, col:col + COL_CHUNK], preferred_element_type=F32)
            dst = slice(c * COL_CHUNK, (c + 1) * COL_CHUNK)
            if dil == 1:
                outs[gi][:, dst] = r.astype(BF16)
            else:
                outs[gi][:, :, dst] = r.reshape(dil, ATT_TM // dil, COL_CHUNK).astype(BF16)


def _attn_kernel(q_ref, kp_ref, kc_ref, vp_ref, vc_ref, o_ref, lse_ref, *, dil):
    nq = ATT_BLOCK
    j = pl.program_id(1)
    q = q_ref[...] * (ATT_HEAD_DIM ** -0.5)
    kk = jnp.concatenate([kp_ref[...], kc_ref[...]], axis=0)
    vv = jnp.concatenate([vp_ref[...], vc_ref[...]], axis=0)
    qi = lax.broadcasted_iota(jnp.int32, (nq, 2 * nq), 0)
    kj = lax.broadcasted_iota(jnp.int32, (nq, 2 * nq), 1)
    dist = qi + nq - kj
    first_key = jnp.where(j >= dil, 0, nq)
    valid = (dist >= 0) & (dist <= nq) & (kj >= first_key)
    pen = jnp.where(valid, (dist * dil).astype(F32), jnp.inf)
    lane = lax.broadcasted_iota(jnp.int32, (1, LANES), 1)
    low_half = lane < ATT_HEAD_DIM
    lane_q = lax.broadcasted_iota(jnp.int32, (nq, LANES), 1)
    lse_all = jnp.zeros((nq, LANES), F32)
    zero_b = jnp.zeros((), BF16)
    for pair in range(ATT_HEADS // 2):
        sl = slice(pair * LANES, (pair + 1) * LANES)
        q_pair, k_pair, v_pair = q[:, sl], kk[:, sl], vv[:, sl]
        outs = []
        for half in range(2):
            h = 2 * pair + half
            slope = 2.0 ** (-8.0 * (h + 1) / ATT_HEADS)
            qm = jnp.where(low_half if half == 0 else jnp.logical_not(low_half), q_pair, zero_b)
            s = lax.dot_general(qm, k_pair, (((1,), (1,)), ((), ())), preferred_element_type=F32)
            s = s - slope * pen
            m = jnp.max(s, axis=-1, keepdims=True)
            p = jnp.exp(s - m)
            denom = jnp.sum(p, axis=-1, keepdims=True)
            pv = jnp.dot(p.astype(BF16), v_pair, preferred_element_type=F32)
            outs.append(pv * (1.0 / denom))
            lse_all = jnp.where(lane_q == h, m + jnp.log(denom), lse_all)
        o_ref[:, sl] = jnp.where(low_half, outs[0], outs[1]).astype(o_ref.dtype)
    lse_ref[...] = lse_all


def _attn_group(qkv_g, bsz, seq, gi, dil):
    t = qkv_g.shape[0]
    assert seq % (dil * ATT_BLOCK) == 0
    nblk = seq // ATT_BLOCK
    blk = (ATT_BLOCK, ATT_COLS)
    cur = lambda c: pl.BlockSpec(blk, lambda b, j: (b * nblk + j, c))
    prev = lambda c: pl.BlockSpec(blk, lambda b, j: (b * nblk + jnp.maximum(j - dil, 0), c))
    return pl.pallas_call(
        functools.partial(_attn_kernel, dil=dil),
        out_shape=[jax.ShapeDtypeStruct((t, ATT_COLS), BF16),
                   jax.ShapeDtypeStruct((t, LANES), F32)],
        grid=(bsz, nblk),
        in_specs=[cur(0), prev(1), cur(1), prev(2), cur(2)],
        out_specs=[pl.BlockSpec(blk, lambda b, j: (b * nblk + j, 0)),
                   pl.BlockSpec((ATT_BLOCK, LANES), lambda b, j: (b * nblk + j, 0))],
        compiler_params=_cparams(2), name=f"dil_attn_g{gi}",
    )(qkv_g, qkv_g, qkv_g, qkv_g, qkv_g)


def _attn_merge_kernel(o0_ref, o1_ref, o2_ref, l0_ref, l1_ref, l2_ref, x_ref, e_ref, wout_ref,
                       lng_ref, lnb_ref, out_ref, oslab_ref, lslab_ref):
    dils = [d for _, d in ATT_GROUPS]
    outs = [o0_ref[...].astype(F32)] + [_interleave_rows(oslab_ref, r, d)
                                        for r, d in ((o1_ref, dils[1]), (o2_ref, dils[2]))]
    l0 = l0_ref[...]
    l1 = _interleave_rows(lslab_ref, l1_ref, dils[1])
    l2 = _interleave_rows(lslab_ref, l2_ref, dils[2])
    m = jnp.maximum(jnp.maximum(l0, l1), l2)
    es = [jnp.exp(l0 - m), jnp.exp(l1 - m), jnp.exp(l2 - m)]
    den = es[0] + es[1] + es[2]
    e_mat = e_ref[...]
    acc = None
    for e, o in zip(es, outs):
        w_x = jnp.dot(_split_hi_lo(e / den), e_mat, preferred_element_type=F32)
        acc = w_x * o if acc is None else acc + w_x * o
    hproj = jnp.dot(acc.astype(BF16), wout_ref[...], preferred_element_type=F32)
    out_ref[...] = _layernorm(DEEPNORM_ALPHA * x_ref[...] + hproj, lng_ref[...], lnb_ref[...])


def _attn_mixer(x2d, bsz, seq, w_qkv, w_out, ln_g, ln_b):
    t = x2d.shape[0]
    n_groups = len(ATT_GROUPS)
    assert n_groups == 3 and ATT_GROUPS[0][1] == 1
    for window, dil in ATT_GROUPS:
        assert window // dil == ATT_BLOCK and seq % (ATT_BLOCK * dil) == 0
        assert ATT_TM % dil == 0 and (ATT_BLOCK * dil) % ATT_TM == 0 or dil == 1
    row = lambda w: pl.BlockSpec((ATT_TM, w), lambda i: (i, 0))

    views, specs = zip(*[_stream_view(jax.ShapeDtypeStruct((t, QKV_COLS), BF16), d)
                         for _, d in ATT_GROUPS[1:]])
    qkvs = pl.pallas_call(
        _qkv_proj_kernel,
        out_shape=[jax.ShapeDtypeStruct((t, QKV_COLS), BF16)]
        + [jax.ShapeDtypeStruct(v, BF16) for v in views],
        grid=(t // ATT_TM,),
        in_specs=[row(D_MODEL), _resident((D_MODEL, n_groups * QKV_COLS))],
        out_specs=[row(QKV_COLS)] + list(specs),
        scratch_shapes=[pltpu.VMEM((ROW_TILE, ATT_TM, LANES), F32)],
        compiler_params=_cparams(1), name="att_qkv_proj",
    )(x2d, w_qkv.astype(BF16))

    outs, lses = [], []
    for gi, (_, dil) in enumerate(ATT_GROUPS):
        o, l = _attn_group(qkvs[gi].reshape(t, QKV_COLS), bsz, seq, gi, dil)
        outs.append(o)
        lses.append(l)

    in_arrays, in_specs = [], []
    for group in (outs, lses):
        for arr, (_, dil) in zip(group, ATT_GROUPS):
            if dil == 1:
                in_arrays.append(arr)
                in_specs.append(row(arr.shape[1]))
            else:
                view, spec = _stream_view(arr, dil)
                in_arrays.append(arr.reshape(view))
                in_specs.append(spec)
    return pl.pallas_call(
        _attn_merge_kernel,
        out_shape=jax.ShapeDtypeStruct((t, D_MODEL), F32),
        grid=(t // ATT_TM,),
        in_specs=in_specs + [row(D_MODEL), _resident((2 * LANES, ATT_COLS)),
                             _resident((ATT_COLS, D_MODEL)),
                             _resident((1, D_MODEL)), _resident((1, D_MODEL))],
        out_specs=row(D_MODEL),
        scratch_shapes=[pltpu.VMEM((ATT_COLS // LANES, ATT_TM, LANES), F32),
                        pltpu.VMEM((1, ATT_TM, LANES), F32)],
        compiler_params=_cparams(1), name="att_merge",
    )(*in_arrays, x2d, _expansion_matrix(ATT_HEADS, ATT_HEAD_DIM), w_out.astype(BF16),
      ln_g.reshape(1, -1), ln_b.reshape(1, -1))


ROUTE_W1, ROUTE_W2, ROUTE_E1, ROUTE_E2, ROUTE_S1, ROUTE_S2 = range(6)
EXPERT_LANE0 = MOE_GROUPS
MOE_TB = 512
RUN_ALIGN = 16
LOCAL_SLOTS = 1536
assert LOCAL_SLOTS >= MOE_TOP_K * MOE_TB + MOE_N_EXPERTS * (RUN_ALIGN - 1)
assert ROUTE_TM == MOE_TB
MAX_PIECES = LOCAL_SLOTS // RUN_ALIGN


def _n_slot_tiles(t):
    n_blocks = t // MOE_TB
    worst = t * MOE_TOP_K + n_blocks * MOE_N_EXPERTS * (RUN_ALIGN - 1) + MOE_N_EXPERTS * (FFN_TM - 1)
    return -(-worst // FFN_TM), (t * MOE_TOP_K) // FFN_TM


def _router_kernel(x_ref, wr_ref, br_ref, route_ref, cnt_ref):
    tm = ROUTE_TM
    logits = _dot_3pass(x_ref[...], wr_ref[...]) + br_ref[...]
    lane_i = lax.broadcasted_iota(jnp.int32, (tm, LANES), 1)
    lane = lane_i.astype(F32)
    big = float(LANES)

    def first_argmax(vals):
        vmax = jnp.max(vals, axis=-1, keepdims=True)
        idx = jnp.min(jnp.where(vals == vmax, lane, big), axis=-1, keepdims=True)
        return vmax, idx

    gl = jnp.where(lane_i < MOE_GROUPS, logits, -jnp.inf)
    gmax, gsel = first_argmax(gl)
    gw = 1.0 / jnp.sum(jnp.exp(gl - gmax), axis=-1, keepdims=True)
    grp_of_lane = jnp.right_shift(lane_i - EXPERT_LANE0, 3).astype(F32)
    in_grp = ((lane_i >= EXPERT_LANE0) & (lane_i < EXPERT_LANE0 + MOE_N_EXPERTS)
              & (grp_of_lane == gsel))
    el = jnp.where(in_grp, logits, -jnp.inf)
    v1, i1 = first_argmax(el)
    v2, i2 = first_argmax(jnp.where(lane == i1, -jnp.inf, el))
    e2 = jnp.exp(v2 - v1)
    w1 = gw / (1.0 + e2)
    w2 = gw * e2 / (1.0 + e2)

    sel1, sel2 = lane == i1, lane == i2
    onehot = jnp.where(sel1 | sel2, 1.0, 0.0)
    r_i = lax.broadcasted_iota(jnp.int32, (tm, tm), 0)
    c_i = lax.broadcasted_iota(jnp.int32, (tm, tm), 1)
    strict = jnp.where(c_i < r_i, 1.0, 0.0).astype(BF16)
    before = jnp.dot(strict, onehot.astype(BF16), preferred_element_type=F32)
    cnt = jnp.sum(onehot, axis=0, keepdims=True)
    run_tiles = jnp.floor((cnt + (RUN_ALIGN - 1)) * (1.0 / RUN_ALIGN))
    e_r = lax.broadcasted_iota(jnp.int32, (LANES, LANES), 0)
    e_c = lax.broadcasted_iota(jnp.int32, (LANES, LANES), 1)
    earlier = jnp.where(e_r < e_c, 1.0, 0.0).astype(BF16)
    run_start = RUN_ALIGN * jnp.dot(jnp.broadcast_to(run_tiles, (SUBLANES, LANES)).astype(BF16),
                                    earlier, preferred_element_type=F32)[0:1, :]
    slot = before + run_start
    slot1 = jnp.sum(jnp.where(sel1, slot, 0.0), axis=-1, keepdims=True)
    slot2 = jnp.sum(jnp.where(sel2, slot, 0.0), axis=-1, keepdims=True)
    cnt_ref[...] = jnp.broadcast_to(cnt, (SUBLANES, LANES))

    out = jnp.zeros((tm, LANES), F32)
    for pos, val in ((ROUTE_W1, w1), (ROUTE_W2, w2), (ROUTE_E1, i1 - EXPERT_LANE0),
                     (ROUTE_E2, i2 - EXPERT_LANE0), (ROUTE_S1, slot1), (ROUTE_S2, slot2)):
        out = jnp.where(lane_i == pos, val, out)
    route_ref[...] = out


def _start_run_copies(tabs, blk, make_copy):
    grow_ref, ntot_ref = tabs
    buf = blk % 2

    def per_piece(j, c):
        g = grow_ref[blk * MAX_PIECES + j]
        make_copy(buf, pl.multiple_of(j * RUN_ALIGN, RUN_ALIGN), pl.multiple_of(g, RUN_ALIGN)).start()
        return c

    lax.fori_loop(0, ntot_ref[blk], per_piece, 0)


def _wait_run_copies(tabs, blk, make_copy):
    buf = blk % 2

    def wait_one(j, c):
        make_copy(buf, 0, 0).wait()
        return c

    lax.fori_loop(0, tabs[1][blk], wait_one, 0)


def _dispatch_kernel(grow_ref, ntot_ref, zrow_ref, zflag_ref,
                     x_ref, route_ref, xs_hbm, loc_ref, zero_ref, sem, zsem):
    i = pl.program_id(0)
    n_zero = zrow_ref.shape[0]

    @pl.when(i == 0)
    def _():
        zero_ref[...] = jnp.zeros_like(zero_ref)

        def zcopy(e):
            start = pl.multiple_of(zrow_ref[e], FFN_TM)
            return pltpu.make_async_copy(zero_ref, xs_hbm.at[pl.ds(start, FFN_TM), :], zsem)

        def zstart(e, carry):
            @pl.when(zflag_ref[e] > 0)
            def _():
                zcopy(e).start()
            return carry

        def zwait(e, carry):
            @pl.when(zflag_ref[e] > 0)
            def _():
                zcopy(e).wait()
            return carry

        lax.fori_loop(0, n_zero, zstart, 0)
        lax.fori_loop(0, n_zero, zwait, 0)

    slots_t = route_ref[...].T
    s1 = slots_t[ROUTE_S1:ROUTE_S1 + 1, :]
    s2 = slots_t[ROUTE_S2:ROUTE_S2 + 1, :]
    xb = x_ref[...].astype(BF16)
    piece = MOE_TB
    buf = i % 2
    for p in range(LOCAL_SLOTS // piece):
        srow = (lax.broadcasted_iota(jnp.int32, (piece, MOE_TB), 0) + p * piece).astype(F32)
        perm = jnp.where((srow == s1) | (srow == s2), 1.0, 0.0).astype(BF16)
        loc_ref[buf, p * piece:(p + 1) * piece, :] = jnp.dot(
            perm, xb, preferred_element_type=F32).astype(BF16)

    def make_copy(b, l, g):
        return pltpu.make_async_copy(loc_ref.at[b, pl.ds(l, RUN_ALIGN), :],
                                     xs_hbm.at[pl.ds(g, RUN_ALIGN), :], sem.at[b])

    tabs = (grow_ref, ntot_ref)
    _start_run_copies(tabs, i, make_copy)

    @pl.when(i > 0)
    def _():
        _wait_run_copies(tabs, i - 1, make_copy)

    @pl.when(i == pl.num_programs(0) - 1)
    def _():
        _wait_run_copies(tabs, i, make_copy)


def _ffn_kernel(te_ref, nused_ref, xs_ref, wg_ref, wu_ref, wd_ref, ys_ref, wgb_ref, wub_ref, wdb_ref):
    i = pl.program_id(0)

    @pl.when((i == 0) | (te_ref[i] != te_ref[jnp.maximum(i - 1, 0)]))
    def _():
        wgb_ref[...] = wg_ref[...].astype(BF16)
        wub_ref[...] = wu_ref[...].astype(BF16)
        wdb_ref[...] = wd_ref[...].astype(BF16)

    @pl.when(i < nused_ref[0])
    def _():
        x = xs_ref[...]
        g = jnp.dot(x, wgb_ref[...], preferred_element_type=F32)
        u = jnp.dot(x, wub_ref[...], preferred_element_type=F32)
        hid = (_silu(g) * u).astype(BF16)
        ys_ref[...] = jnp.dot(hid, wdb_ref[...], preferred_element_type=F32).astype(ys_ref.dtype)

    @pl.when(i >= nused_ref[0])
    def _():
        ys_ref[...] = jnp.zeros_like(ys_ref)


def _combine_kernel(grow_ref, ntot_ref, ys_hbm, route_ref, x_ref,
                    lng_ref, lnb_ref, o_ref, loc_ref, sem):
    i = pl.program_id(0)

    def make_copy(b, l, g):
        return pltpu.make_async_copy(ys_hbm.at[pl.ds(g, RUN_ALIGN), :],
                                     loc_ref.at[b, pl.ds(l, RUN_ALIGN), :], sem.at[b])

    tabs = (grow_ref, ntot_ref)

    @pl.when(i == 0)
    def _():
        loc_ref[...] = jnp.zeros_like(loc_ref)
        _start_run_copies(tabs, i, make_copy)

    @pl.when(i + 1 < pl.num_programs(0))
    def _():
        _start_run_copies(tabs, i + 1, make_copy)

    _wait_run_copies(tabs, i, make_copy)

    route = route_ref[...]
    w1, w2 = route[:, ROUTE_W1:ROUTE_W1 + 1], route[:, ROUTE_W2:ROUTE_W2 + 1]
    s1, s2 = route[:, ROUTE_S1:ROUTE_S1 + 1], route[:, ROUTE_S2:ROUTE_S2 + 1]
    piece = MOE_TB
    buf = i % 2
    f = None
    for p in range(LOCAL_SLOTS // piece):
        scol = (lax.broadcasted_iota(jnp.int32, (MOE_TB, piece), 1) + p * piece).astype(F32)
        comb = (jnp.where(scol == s1, w1, 0.0) + jnp.where(scol == s2, w2, 0.0)).astype(BF16)
        part = jnp.dot(comb, loc_ref[buf, p * piece:(p + 1) * piece, :], preferred_element_type=F32)
        f = part if f is None else f + part
    o_ref[...] = _layernorm(DEEPNORM_ALPHA * x_ref[...] + f, lng_ref[...], lnb_ref[...])


def _moe_layer(x2d, layer, wg, bg, we, be, w_gate, w_up, w_down, ln_g, ln_b):
    t = x2d.shape[0]
    assert t % MOE_TB == 0
    n_blocks = t // MOE_TB
    n_tiles, min_tiles = _n_slot_tiles(t)
    n_slots = n_tiles * FFN_TM

    pad_cols = LANES - MOE_GROUPS - MOE_N_EXPERTS
    wr = jnp.pad(jnp.concatenate([wg, we], axis=1), ((0, 0), (0, pad_cols)))
    br = jnp.pad(jnp.concatenate([bg, be]), (0, pad_cols)).reshape(1, LANES)
    route, counts = pl.pallas_call(
        _router_kernel,
        out_shape=[jax.ShapeDtypeStruct((t, LANES), F32),
                   jax.ShapeDtypeStruct((n_blocks * SUBLANES, LANES), F32)],
        grid=(n_blocks,),
        in_specs=[pl.BlockSpec((ROUTE_TM, D_MODEL), lambda i: (i, 0)),
                  _resident((D_MODEL, LANES)), _resident((1, LANES))],
        out_specs=[pl.BlockSpec((ROUTE_TM, LANES), lambda i: (i, 0)),
                   pl.BlockSpec((SUBLANES, LANES), lambda i: (i, 0))],
        compiler_params=_cparams(1), name="moe_router",
    )(x2d, wr, br)

    cnt = counts.reshape(n_blocks, SUBLANES, LANES)[:, 0, EXPERT_LANE0:EXPERT_LANE0 + MOE_N_EXPERTS]
    cnt = cnt.astype(jnp.int32)
    run = ((cnt + RUN_ALIGN - 1) // RUN_ALIGN) * RUN_ALIGN
    lstart = jnp.cumsum(run, axis=1) - run
    seg_len = jnp.sum(run, axis=0)
    seg_pad = ((seg_len + FFN_TM - 1) // FFN_TM) * FFN_TM
    ends = jnp.cumsum(seg_pad)
    gstart = (ends - seg_pad)[None, :] + jnp.cumsum(run, axis=0) - run
    piece_row = jnp.arange(MAX_PIECES, dtype=jnp.int32)[None, :, None] * RUN_ALIGN
    lend = (lstart + run)[:, None, :]
    owns = (lstart[:, None, :] <= piece_row) & (piece_row < lend)
    grow = jnp.sum(jnp.where(owns, (gstart - lstart)[:, None, :], 0), axis=2) + piece_row[:, :, 0]
    tabs = [grow.astype(jnp.int32).reshape(-1),
            (jnp.sum(run, axis=1) // RUN_ALIGN).astype(jnp.int32)]
    tile_start = jnp.arange(n_tiles, dtype=jnp.int32) * FFN_TM
    tile_expert = jnp.minimum(jnp.sum(ends[None, :] <= tile_start[:, None], axis=1),
                              MOE_N_EXPERTS - 1).astype(jnp.int32)
    n_used = (ends[-1] // FFN_TM).astype(jnp.int32).reshape(1)
    tail_tile = jnp.arange(min_tiles, n_tiles, dtype=jnp.int32)
    zrow = jnp.concatenate([jnp.maximum(ends - FFN_TM, 0), tail_tile * FFN_TM]).astype(jnp.int32)
    zflag = jnp.concatenate([seg_pad > 0, tail_tile >= n_used[0]]).astype(jnp.int32)

    blk_row = lambda w: pl.BlockSpec((MOE_TB, w), lambda i, *_: (i, 0))
    xs = pl.pallas_call(
        _dispatch_kernel,
        out_shape=jax.ShapeDtypeStruct((n_slots, D_MODEL), BF16),
        grid_spec=pltpu.PrefetchScalarGridSpec(
            num_scalar_prefetch=4, grid=(n_blocks,),
            in_specs=[blk_row(D_MODEL), blk_row(LANES)],
            out_specs=pl.BlockSpec(memory_space=pl.ANY),
            scratch_shapes=[pltpu.VMEM((2, LOCAL_SLOTS, D_MODEL), BF16),
                            pltpu.VMEM((FFN_TM, D_MODEL), BF16),
                            pltpu.SemaphoreType.DMA((2,)), pltpu.SemaphoreType.DMA]),
        compiler_params=_cparams(1), name="moe_dispatch",
    )(*tabs, zrow, zflag, x2d, route)

    w_spec = lambda r, c: pl.BlockSpec((None, None, r, c), lambda i, te, nu: (layer, te[i], 0, 0))
    slot_blk = pl.BlockSpec((FFN_TM, D_MODEL), lambda i, te, nu: (i, 0))
    used_blk = pl.BlockSpec((FFN_TM, D_MODEL), lambda i, te, nu: (jnp.minimum(i, nu[0] - 1), 0))
    ys = pl.pallas_call(
        _ffn_kernel,
        out_shape=jax.ShapeDtypeStruct((n_slots, D_MODEL), BF16),
        grid_spec=pltpu.PrefetchScalarGridSpec(
            num_scalar_prefetch=2, grid=(n_tiles,),
            in_specs=[used_blk, w_spec(D_MODEL, MOE_D_EXPERT), w_spec(D_MODEL, MOE_D_EXPERT),
                      w_spec(MOE_D_EXPERT, D_MODEL)],
            out_specs=slot_blk,
            scratch_shapes=[pltpu.VMEM((D_MODEL, MOE_D_EXPERT), BF16),
                            pltpu.VMEM((D_MODEL, MOE_D_EXPERT), BF16),
                            pltpu.VMEM((MOE_D_EXPERT, D_MODEL), BF16)]),
        compiler_params=_cparams(1), name="moe_ffn",
    )(tile_expert, n_used, xs, w_gate, w_up, w_down)

    vec = pl.BlockSpec((1, D_MODEL), lambda i, *_: (0, 0))
    return pl.pallas_call(
        _combine_kernel,
        out_shape=jax.ShapeDtypeStruct((t, D_MODEL), F32),
        grid_spec=pltpu.PrefetchScalarGridSpec(
            num_scalar_prefetch=2, grid=(n_blocks,),
            in_specs=[pl.BlockSpec(memory_space=pl.ANY), blk_row(LANES), blk_row(D_MODEL), vec, vec],
            out_specs=blk_row(D_MODEL),
            scratch_shapes=[pltpu.VMEM((2, LOCAL_SLOTS, D_MODEL), BF16),
                            pltpu.SemaphoreType.DMA((2,))]),
        compiler_params=_cparams(1), name="moe_combine",
    )(*tabs, ys, route, x2d, ln_g.reshape(1, -1), ln_b.reshape(1, -1))


def kernel(x, ssd_w_in, ssd_conv_w, ssd_conv_b, ssd_dt_bias, ssd_a_log, ssd_d, ssd_norm_w, ssd_w_out,
           sc_w_in, sc_conv_w, sc_w_out, att_w_qkv, att_w_out,
           moe_wg, moe_bg, moe_we, moe_be, moe_w_gate, moe_w_up, moe_w_down, ln_g, ln_b):
    bsz, seq, d = x.shape
    assert d == D_MODEL
    h = x.reshape(bsz * seq, d)
    for i in range(DEPTH):
        kind, j = i % N_MIXERS, i // N_MIXERS
        if kind == 0:
            h = _ssd_mixer(h, bsz, seq, ssd_w_in[j], ssd_conv_w[j], ssd_conv_b[j], ssd_dt_bias[j],
                           ssd_a_log[j], ssd_d[j], ssd_norm_w[j], ssd_w_out[j], ln_g[i, 0], ln_b[i, 0])
        elif kind == 1:
            h = _sconv_mixer(h, bsz, seq, sc_w_in[j], sc_conv_w[j], sc_w_out[j], ln_g[i, 0], ln_b[i, 0])
        else:
            h = _attn_mixer(h, bsz, seq, att_w_qkv[j], att_w_out[j], ln_g[i, 0], ln_b[i, 0])
        h = _moe_layer(h, i, moe_wg[i], moe_bg[i], moe_we[i], moe_be[i],
                       moe_w_gate, moe_w_up, moe_w_down, ln_g[i, 1], ln_b[i, 1])
    return h.reshape(bsz, seq, d)
```

```python
import functools

import numpy as np
import jax
import jax.numpy as jnp
from jax import lax
from jax.experimental import pallas as pl
from jax.experimental.pallas import tpu as pltpu

D_MODEL = 1024
DEPTH = 4
N_MIXERS = 3
DEEPNORM_ALPHA = (2 * DEPTH) ** 0.25
LN_EPS = 1e-5

SSD_D_INNER = 2 * D_MODEL
SSD_HEAD_DIM = 64
SSD_N_HEADS = SSD_D_INNER // SSD_HEAD_DIM
SSD_N_GROUPS = 4
SSD_D_STATE = 128
SSD_CONV_WIDTH = 4
SSD_CHUNK = 256
SSD_SUBCHUNK = 128
SSD_CONV_DIM = SSD_D_INNER + 2 * SSD_N_GROUPS * SSD_D_STATE
SSD_HEADS_PER_GROUP = SSD_N_HEADS // SSD_N_GROUPS
SSD_GROUP_WIDTH = SSD_D_INNER // SSD_N_GROUPS

SC_WIDTH = 3

ATT_GROUPS = ((128, 1), (512, 4), (2048, 16))
ATT_HEAD_DIM = 64
ATT_HEADS = D_MODEL // ATT_HEAD_DIM
ATT_BLOCK = 128
ATT_COLS = ATT_HEADS * ATT_HEAD_DIM

MOE_GROUPS = 4
MOE_EXPERTS_PER_GROUP = 8
MOE_N_EXPERTS = MOE_GROUPS * MOE_EXPERTS_PER_GROUP
MOE_TOP_K = 2
MOE_D_EXPERT = 256

LANES = 128
SUBLANES = 8
VMEM_LIMIT_BYTES = 56 * 1024 * 1024

PROJ_TM = 512
CONV_TM = 512
MERGE_TM = 512
ROUTE_TM = 512
FFN_TM = 512
COL_CHUNK = 512
ROW_TILE = D_MODEL // LANES

F32 = jnp.float32
BF16 = jnp.bfloat16
HIGHEST = lax.Precision.HIGHEST


def _cparams(n_axes):
    return pltpu.CompilerParams(dimension_semantics=("arbitrary",) * n_axes,
                                vmem_limit_bytes=VMEM_LIMIT_BYTES)


def _resident(shape):
    nd = len(shape)
    return pl.BlockSpec(shape, lambda *_: (0,) * nd, pipeline_mode=pl.Buffered(1))


def _silu(v):
    return v * (1.0 / (1.0 + jnp.exp(-v)))


def _softplus(v):
    return jnp.maximum(v, 0.0) + jnp.log1p(jnp.exp(-jnp.abs(v)))


def _layernorm(v, g, b):
    mu = jnp.mean(v, axis=-1, keepdims=True)
    d = v - mu
    var = jnp.mean(d * d, axis=-1, keepdims=True)
    return d * lax.rsqrt(var + LN_EPS) * g + b


def _split_hi_lo(v):
    hi = v.astype(BF16)
    lo = (v - hi.astype(F32)).astype(BF16)
    return jnp.concatenate([hi, lo], axis=1)


def _expansion_matrix(n_heads, width):
    e = np.zeros((2 * LANES, n_heads * width), np.float32)
    for h in range(n_heads):
        e[h, h * width:(h + 1) * width] = 1.0
        e[LANES + h, h * width:(h + 1) * width] = 1.0
    return jnp.asarray(e, dtype=BF16)


def _dot_3pass(x, w):
    xh = x.astype(BF16)
    xl = (x - xh.astype(F32)).astype(BF16)
    wh = w.astype(BF16)
    wl = (w - wh.astype(F32)).astype(BF16)
    return (jnp.dot(xh, wh, preferred_element_type=F32) + jnp.dot(xl, wh, preferred_element_type=F32)
            + jnp.dot(xh, wl, preferred_element_type=F32))


def _proj_kernel(x_ref, w_ref, wdt_ref, *outs, widths):
    x = x_ref[...]
    xb = x.astype(BF16)
    off = 0
    for j, n in enumerate(widths):
        for c in range(n // COL_CHUNK):
            r = jnp.dot(xb, w_ref[:, off + c * COL_CHUNK: off + (c + 1) * COL_CHUNK],
                        preferred_element_type=F32)
            outs[j][:, c * COL_CHUNK:(c + 1) * COL_CHUNK] = r.astype(outs[j].dtype)
        off += n
    outs[len(widths)][...] = _dot_3pass(x, wdt_ref[...])


def _project(x, w_bf16, widths, w_dt, name):
    t, k = x.shape
    n_total = sum(widths)
    assert w_bf16.shape == (k, n_total) and t % PROJ_TM == 0
    row = lambda n: pl.BlockSpec((PROJ_TM, n), lambda i: (i, 0))
    return pl.pallas_call(
        functools.partial(_proj_kernel, widths=tuple(widths)),
        out_shape=[jax.ShapeDtypeStruct((t, n), BF16) for n in widths]
        + [jax.ShapeDtypeStruct((t, w_dt.shape[1]), F32)],
        grid=(t // PROJ_TM,),
        in_specs=[row(k), _resident((k, n_total)), _resident(w_dt.shape)],
        out_specs=[row(n) for n in widths] + [row(w_dt.shape[1])],
        compiler_params=_cparams(1), name=name)(x, w_bf16, w_dt)


def _ssd_kernel(z_ref, xbc_ref, dt_ref, x_ref, convw_ref, convb_ref, dtb_ref, alog_ref,
                dskip_ref, normw_ref, wout_ref, lng_ref, lnb_ref, e_ref,
                o_ref, state_ref, carry_ref, y_ref, act_ref):
    L = SSD_CHUNK
    SUB = SSD_SUBCHUNK
    N = SSD_D_STATE
    GW = SSD_GROUP_WIDTH
    c_idx = pl.program_id(1)

    @pl.when(c_idx == 0)
    def _():
        state_ref[...] = jnp.zeros_like(state_ref)
        carry_ref[...] = jnp.zeros_like(carry_ref)

    row = lax.broadcasted_iota(jnp.int32, (L, L), 0)
    col = lax.broadcasted_iota(jnp.int32, (L, L), 1)
    shift_mats = [jnp.where(row - col == k, 1.0, 0.0).astype(BF16) for k in range(1, SSD_CONV_WIDTH)]
    row8 = lax.broadcasted_iota(jnp.int32, (SUBLANES, COL_CHUNK), 0)
    for c0 in range(0, SSD_CONV_DIM, COL_CHUNK):
        cs_ = slice(c0, c0 + COL_CHUNK)
        u_b = xbc_ref[:, cs_]
        u = u_b.astype(F32)
        cw = convw_ref[:, cs_]
        carry = carry_ref[:, cs_]
        acc = convb_ref[:, cs_] + cw[SSD_CONV_WIDTH - 1:SSD_CONV_WIDTH, :] * u
        head = jnp.zeros((SUBLANES, COL_CHUNK), F32)
        for k in range(1, SSD_CONV_WIDTH):
            wk = cw[SSD_CONV_WIDTH - 1 - k:SSD_CONV_WIDTH - k, :]
            acc = acc + wk * jnp.dot(shift_mats[k - 1], u_b, preferred_element_type=F32)
            head = head + wk * jnp.where(row8 < k, pltpu.roll(carry, k, 0), 0.0)
        carry_ref[:, cs_] = u[L - SUBLANES:, :]
        acc = jnp.concatenate([acc[:SUBLANES] + head, acc[SUBLANES:]], axis=0)
        act_ref[:, cs_] = _silu(acc)
    xs = act_ref[:, :SSD_D_INNER]
    bmat = act_ref[:, SSD_D_INNER:SSD_D_INNER + SSD_N_GROUPS * N]
    cmat = act_ref[:, SSD_D_INNER + SSD_N_GROUPS * N:]

    dt = _softplus(dt_ref[...] + dtb_ref[...])
    a = -jnp.exp(alog_ref[...])
    dta = dt * a
    rs = lax.broadcasted_iota(jnp.int32, (SUB, SUB), 0)
    cs_i = lax.broadcasted_iota(jnp.int32, (SUB, SUB), 1)
    causal = rs >= cs_i
    causal_f = causal.astype(F32)
    e_mat = e_ref[...]
    lane = lax.broadcasted_iota(jnp.int32, (1, LANES), 1)
    low_half = lane < SSD_HEAD_DIM
    dskip = dskip_ref[...]

    for sc in range(L // SUB):
        rows = slice(sc * SUB, (sc + 1) * SUB)
        dt_s = dt[rows]
        cs = jnp.dot(causal_f, dta[rows], precision=HIGHEST, preferred_element_type=F32)
        cs_t = cs.T
        last = cs[SUB - 1:SUB, :]
        xs_s = xs[rows]
        ecs_x = jnp.dot(_split_hi_lo(jnp.exp(cs)), e_mat, preferred_element_type=F32)
        ws_x = jnp.dot(_split_hi_lo(jnp.exp(last - cs) * dt_s), e_mat, preferred_element_type=F32)
        dt_x = jnp.dot(_split_hi_lo(dt_s), e_mat, preferred_element_type=F32)
        elast_x = jnp.dot(_split_hi_lo(jnp.broadcast_to(jnp.exp(last), (SUBLANES, LANES))), e_mat,
                          preferred_element_type=F32)[0:1, :]
        xw_b = (xs_s * ws_x).astype(BF16)
        xdt_b = (xs_s * dt_x).astype(BF16)
        for g in range(SSD_N_GROUPS):
            bg = bmat[rows, g * N:(g + 1) * N]
            cg_b = cmat[rows, g * N:(g + 1) * N].astype(BF16)
            cb = lax.dot_general(cg_b, bg.astype(BF16), (((1,), (1,)), ((), ())),
                                 preferred_element_type=F32)
            s_old = state_ref[g]
            yoff = jnp.dot(cg_b, s_old.astype(BF16), preferred_element_type=F32)
            for pr in range(SSD_HEADS_PER_GROUP // 2):
                pair = g * (SSD_HEADS_PER_GROUP // 2) + pr
                cols = slice(pair * LANES, (pair + 1) * LANES)
                x_pair = xdt_b[:, cols]
                ys = []
                for half in range(2):
                    h = 2 * pair + half
                    seg = cs[:, h:h + 1] - cs_t[h:h + 1, :]
                    mix = cb * jnp.exp(jnp.where(causal, seg, -jnp.inf))
                    ys.append(jnp.dot(mix.astype(BF16), x_pair, preferred_element_type=F32))
                off_cols = slice(pr * LANES, (pr + 1) * LANES)
                y_ref[rows, cols] = (jnp.where(low_half, ys[0], ys[1])
                                     + yoff[:, off_cols] * ecs_x[:, cols] + dskip[:, cols] * xs_s[:, cols])
            upd = jnp.dot(bg.T.astype(BF16), xw_b[:, g * GW:(g + 1) * GW], preferred_element_type=F32)
            state_ref[g] = s_old * elast_x[:, g * GW:(g + 1) * GW] + upd

    y = y_ref[...] * _silu(z_ref[...].astype(F32))
    normw = normw_ref[...]
    yn_parts = []
    for g in range(SSD_N_GROUPS):
        yg = y[:, g * GW:(g + 1) * GW]
        ms = jnp.mean(yg * yg, axis=-1, keepdims=True)
        yn_parts.append((yg * lax.rsqrt(ms + LN_EPS) * normw[:, g * GW:(g + 1) * GW]).astype(BF16))
    yn = jnp.concatenate(yn_parts, axis=1)
    hproj = jnp.dot(yn, wout_ref[...], preferred_element_type=F32)
    o_ref[...] = _layernorm(DEEPNORM_ALPHA * x_ref[...] + hproj, lng_ref[...], lnb_ref[...])


def _ssd_mixer(x2d, bsz, seq, w_in, conv_w, conv_b, dt_bias, a_log, d_skip, norm_w, w_out,
               ln_g, ln_b):
    t = x2d.shape[0]
    nc = seq // SSD_CHUNK
    assert seq % SSD_CHUNK == 0
    n_main = SSD_D_INNER + SSD_CONV_DIM
    w_main = w_in[:, :n_main].astype(BF16)
    w_dt = jnp.pad(w_in[:, n_main:], ((0, 0), (0, LANES - SSD_N_HEADS)))
    z, xbc, dt_raw = _project(x2d, w_main, (SSD_D_INNER, SSD_CONV_DIM), w_dt, name="ssd_in_proj")
    pad = (0, LANES - SSD_N_HEADS)
    dtb = jnp.pad(dt_bias, pad).reshape(1, LANES)
    alog = jnp.pad(a_log, pad).reshape(1, LANES)
    dskip = jnp.repeat(d_skip, SSD_HEAD_DIM).reshape(1, SSD_D_INNER)
    L = SSD_CHUNK
    blk = lambda w: pl.BlockSpec((L, w), lambda b, c: (b * nc + c, 0))
    return pl.pallas_call(
        _ssd_kernel,
        out_shape=jax.ShapeDtypeStruct((t, D_MODEL), F32),
        grid=(bsz, nc),
        in_specs=[blk(SSD_D_INNER), blk(SSD_CONV_DIM), blk(LANES), blk(D_MODEL),
                  _resident((SSD_CONV_WIDTH, SSD_CONV_DIM)), _resident((1, SSD_CONV_DIM)),
                  _resident((1, LANES)), _resident((1, LANES)),
                  _resident((1, SSD_D_INNER)), _resident((1, SSD_D_INNER)),
                  _resident((SSD_D_INNER, D_MODEL)),
                  _resident((1, D_MODEL)), _resident((1, D_MODEL)),
                  _resident((2 * LANES, SSD_D_INNER))],
        out_specs=blk(D_MODEL),
        scratch_shapes=[pltpu.VMEM((SSD_N_GROUPS, SSD_D_STATE, SSD_GROUP_WIDTH), F32),
                        pltpu.VMEM((SUBLANES, SSD_CONV_DIM), F32),
                        pltpu.VMEM((L, SSD_D_INNER), F32),
                        pltpu.VMEM((L, SSD_CONV_DIM), F32)],
        compiler_params=_cparams(2), name="ssd_scan",
    )(z, xbc, dt_raw, x2d, conv_w, conv_b.reshape(1, -1), dtb, alog, dskip,
      norm_w.reshape(1, -1), w_out.astype(BF16), ln_g.reshape(1, -1), ln_b.reshape(1, -1),
      _expansion_matrix(SSD_N_HEADS, SSD_HEAD_DIM))


def _sconv_kernel(x_ref, win_ref, convw_ref, wout_ref, lng_ref, lnb_ref, o_ref, vbuf_ref):
    tm = CONV_TM

    @pl.when(pl.program_id(1) == 0)
    def _():
        vbuf_ref[0:SUBLANES, :] = jnp.zeros((SUBLANES, D_MODEL), F32)

    x = x_ref[...]
    xb = x.astype(BF16)
    gb = jnp.dot(xb, win_ref[:, 0:D_MODEL], preferred_element_type=F32)
    gc = jnp.dot(xb, win_ref[:, D_MODEL:2 * D_MODEL], preferred_element_type=F32)
    u = jnp.dot(xb, win_ref[:, 2 * D_MODEL:3 * D_MODEL], preferred_element_type=F32)
    v = gc * u
    vbuf_ref[SUBLANES:SUBLANES + tm, :] = v
    cw = convw_ref[...]
    conv = cw[SC_WIDTH - 1:SC_WIDTH, :] * v
    for k in range(SC_WIDTH - 1):
        shift = SC_WIDTH - 1 - k
        conv = conv + cw[k:k + 1, :] * vbuf_ref[SUBLANES - shift:SUBLANES - shift + tm, :]
    vbuf_ref[0:SUBLANES, :] = vbuf_ref[tm:tm + SUBLANES, :]
    y = (gb * conv).astype(BF16)
    hproj = jnp.dot(y, wout_ref[...], preferred_element_type=F32)
    o_ref[...] = _layernorm(DEEPNORM_ALPHA * x + hproj, lng_ref[...], lnb_ref[...])


def _sconv_mixer(x2d, bsz, seq, w_in, conv_w, w_out, ln_g, ln_b):
    t = x2d.shape[0]
    ns = seq // CONV_TM
    assert seq % CONV_TM == 0
    blk = pl.BlockSpec((CONV_TM, D_MODEL), lambda b, s: (b * ns + s, 0))
    return pl.pallas_call(
        _sconv_kernel,
        out_shape=jax.ShapeDtypeStruct((t, D_MODEL), F32),
        grid=(bsz, ns),
        in_specs=[blk, _resident((D_MODEL, 3 * D_MODEL)), _resident((SC_WIDTH, D_MODEL)),
                  _resident((D_MODEL, D_MODEL)), _resident((1, D_MODEL)), _resident((1, D_MODEL))],
        out_specs=blk,
        scratch_shapes=[pltpu.VMEM((CONV_TM + SUBLANES, D_MODEL), F32)],
        compiler_params=_cparams(2), name="sconv_mixer",
    )(x2d, w_in.astype(BF16), conv_w, w_out.astype(BF16), ln_g.reshape(1, -1), ln_b.reshape(1, -1))


ATT_TM = 512
QKV_COLS = 3 * ATT_COLS


def _stream_view(arr, dil):
    t, c = arr.shape
    span = ATT_BLOCK * dil
    per_tile = ATT_TM // dil
    tiles_per_span = span // ATT_TM
    view = (t // span, dil, ATT_BLOCK, c)
    spec = pl.BlockSpec((None, dil, per_tile, c),
                        lambda i: (i // tiles_per_span, 0, i % tiles_per_span, 0))
    return view, spec


def _deinterleave_rows(slab_ref, dil):
    n_i = ATT_TM // dil
    rows = [jnp.concatenate([slab_ref[s, pl.ds(r, n_i, stride=dil), :] for s in range(ROW_TILE)],
                            axis=1) for r in range(dil)]
    return jnp.concatenate(rows, axis=0).astype(BF16)


def _interleave_rows(slab_ref, blk_ref, dil):
    n_i, c = blk_ref.shape[1], blk_ref.shape[2]
    for r in range(dil):
        br = blk_ref[r].astype(F32)
        for s in range(c // LANES):
            slab_ref[s, pl.ds(r, n_i, stride=dil), :] = br[:, s * LANES:(s + 1) * LANES]
    return jnp.concatenate([slab_ref[s] for s in range(c // LANES)], axis=1)


def _qkv_proj_kernel(x_ref, w_ref, o0_ref, o1_ref, o2_ref, slab_ref):
    x = x_ref[...]
    for s in range(ROW_TILE):
        slab_ref[s] = x[:, s * LANES:(s + 1) * LANES]
    outs = (o0_ref, o1_ref, o2_ref)
    for gi, (_, dil) in enumerate(ATT_GROUPS):
        xb = x.astype(BF16) if dil == 1 else _deinterleave_rows(slab_ref, dil)
        for c in range(QKV_COLS // COL_CHUNK):
            col = gi * QKV_COLS + c * COL_CHUNK
            r = jnp.dot(xb, w_ref[:, col:col + COL_CHUNK], preferred_element_type=F32)
            dst = slice(c * COL_CHUNK, (c + 1) * COL_CHUNK)
            if dil == 1:
                outs[gi][:, dst] = r.astype(BF16)
            else:
                outs[gi][:, :, dst] = r.reshape(dil, ATT_TM // dil, COL_CHUNK).astype(BF16)


def _attn_kernel(q_ref, kp_ref, kc_ref, vp_ref, vc_ref, o_ref, lse_ref, *, dil):
    nq = ATT_BLOCK
    j = pl.program_id(1)
    q = q_ref[...] * (ATT_HEAD_DIM ** -0.5)
    kk = jnp.concatenate([kp_ref[...], kc_ref[...]], axis=0)
    vv = jnp.concatenate([vp_ref[...], vc_ref[...]], axis=0)
    qi = lax.broadcasted_iota(jnp.int32, (nq, 2 * nq), 0)
    kj = lax.broadcasted_iota(jnp.int32, (nq, 2 * nq), 1)
    dist = qi + nq - kj
    first_key = jnp.where(j >= dil, 0, nq)
    valid = (dist >= 0) & (dist <= nq) & (kj >= first_key)
    pen = jnp.where(valid, (dist * dil).astype(F32), jnp.inf)
    lane = lax.broadcasted_iota(jnp.int32, (1, LANES), 1)
    low_half = lane < ATT_HEAD_DIM
    lane_q = lax.broadcasted_iota(jnp.int32, (nq, LANES), 1)
    lse_all = jnp.zeros((nq, LANES), F32)
    zero_b = jnp.zeros((), BF16)
    for pair in range(ATT_HEADS // 2):
        sl = slice(pair * LANES, (pair + 1) * LANES)
        q_pair, k_pair, v_pair = q[:, sl], kk[:, sl], vv[:, sl]
        outs = []
        for half in range(2):
            h = 2 * pair + half
            slope = 2.0 ** (-8.0 * (h + 1) / ATT_HEADS)
            qm = jnp.where(low_half if half == 0 else jnp.logical_not(low_half), q_pair, zero_b)
            s = lax.dot_general(qm, k_pair, (((1,), (1,)), ((), ())), preferred_element_type=F32)
            s = s - slope * pen
            m = jnp.max(s, axis=-1, keepdims=True)
            p = jnp.exp(s - m)
            denom = jnp.sum(p, axis=-1, keepdims=True)
            pv = jnp.dot(p.astype(BF16), v_pair, preferred_element_type=F32)
            outs.append(pv * (1.0 / denom))
            lse_all = jnp.where(lane_q == h, m + jnp.log(denom), lse_all)
        o_ref[:, sl] = jnp.where(low_half, outs[0], outs[1]).astype(o_ref.dtype)
    lse_ref[...] = lse_all


def _attn_group(qkv_g, bsz, seq, gi, dil):
    t = qkv_g.shape[0]
    assert seq % (dil * ATT_BLOCK) == 0
    nblk = seq // ATT_BLOCK
    blk = (ATT_BLOCK, ATT_COLS)
    cur = lambda c: pl.BlockSpec(blk, lambda b, j: (b * nblk + j, c))
    prev = lambda c: pl.BlockSpec(blk, lambda b, j: (b * nblk + jnp.maximum(j - dil, 0), c))
    return pl.pallas_call(
        functools.partial(_attn_kernel, dil=dil),
        out_shape=[jax.ShapeDtypeStruct((t, ATT_COLS), BF16),
                   jax.ShapeDtypeStruct((t, LANES), F32)],
        grid=(bsz, nblk),
        in_specs=[cur(0), prev(1), cur(1), prev(2), cur(2)],
        out_specs=[pl.BlockSpec(blk, lambda b, j: (b * nblk + j, 0)),
                   pl.BlockSpec((ATT_BLOCK, LANES), lambda b, j: (b * nblk + j, 0))],
        compiler_params=_cparams(2), name=f"dil_attn_g{gi}",
    )(qkv_g, qkv_g, qkv_g, qkv_g, qkv_g)


def _attn_merge_kernel(o0_ref, o1_ref, o2_ref, l0_ref, l1_ref, l2_ref, x_ref, e_ref, wout_ref,
                       lng_ref, lnb_ref, out_ref, oslab_ref, lslab_ref):
    dils = [d for _, d in ATT_GROUPS]
    outs = [o0_ref[...].astype(F32)] + [_interleave_rows(oslab_ref, r, d)
                                        for r, d in ((o1_ref, dils[1]), (o2_ref, dils[2]))]
    l0 = l0_ref[...]
    l1 = _interleave_rows(lslab_ref, l1_ref, dils[1])
    l2 = _interleave_rows(lslab_ref, l2_ref, dils[2])
    m = jnp.maximum(jnp.maximum(l0, l1), l2)
    es = [jnp.exp(l0 - m), jnp.exp(l1 - m), jnp.exp(l2 - m)]
    den = es[0] + es[1] + es[2]
    e_mat = e_ref[...]
    acc = None
    for e, o in zip(es, outs):
        w_x = jnp.dot(_split_hi_lo(e / den), e_mat, preferred_element_type=F32)
        acc = w_x * o if acc is None else acc + w_x * o
    hproj = jnp.dot(acc.astype(BF16), wout_ref[...], preferred_element_type=F32)
    out_ref[...] = _layernorm(DEEPNORM_ALPHA * x_ref[...] + hproj, lng_ref[...], lnb_ref[...])


def _attn_mixer(x2d, bsz, seq, w_qkv, w_out, ln_g, ln_b):
    t = x2d.shape[0]
    n_groups = len(ATT_GROUPS)
    assert n_groups == 3 and ATT_GROUPS[0][1] == 1
    for window, dil in ATT_GROUPS:
        assert window // dil == ATT_BLOCK and seq % (ATT_BLOCK * dil) == 0
        assert ATT_TM % dil == 0 and (ATT_BLOCK * dil) % ATT_TM == 0 or dil == 1
    row = lambda w: pl.BlockSpec((ATT_TM, w), lambda i: (i, 0))

    views, specs = zip(*[_stream_view(jax.ShapeDtypeStruct((t, QKV_COLS), BF16), d)
                         for _, d in ATT_GROUPS[1:]])
    qkvs = pl.pallas_call(
        _qkv_proj_kernel,
        out_shape=[jax.ShapeDtypeStruct((t, QKV_COLS), BF16)]
        + [jax.ShapeDtypeStruct(v, BF16) for v in views],
        grid=(t // ATT_TM,),
        in_specs=[row(D_MODEL), _resident((D_MODEL, n_groups * QKV_COLS))],
        out_specs=[row(QKV_COLS)] + list(specs),
        scratch_shapes=[pltpu.VMEM((ROW_TILE, ATT_TM, LANES), F32)],
        compiler_params=_cparams(1), name="att_qkv_proj",
    )(x2d, w_qkv.astype(BF16))

    outs, lses = [], []
    for gi, (_, dil) in enumerate(ATT_GROUPS):
        o, l = _attn_group(qkvs[gi].reshape(t, QKV_COLS), bsz, seq, gi, dil)
        outs.append(o)
        lses.append(l)

    in_arrays, in_specs = [], []
    for group in (outs, lses):
        for arr, (_, dil) in zip(group, ATT_GROUPS):
            if dil == 1:
                in_arrays.append(arr)
                in_specs.append(row(arr.shape[1]))
            else:
                view, spec = _stream_view(arr, dil)
                in_arrays.append(arr.reshape(view))
                in_specs.append(spec)
    return pl.pallas_call(
        _attn_merge_kernel,
        out_shape=jax.ShapeDtypeStruct((t, D_MODEL), F32),
        grid=(t // ATT_TM,),
        in_specs=in_specs + [row(D_MODEL), _resident((2 * LANES, ATT_COLS)),
                             _resident((ATT_COLS, D_MODEL)),
                             _resident((1, D_MODEL)), _resident((1, D_MODEL))],
        out_specs=row(D_MODEL),
        scratch_shapes=[pltpu.VMEM((ATT_COLS // LANES, ATT_TM, LANES), F32),
                        pltpu.VMEM((1, ATT_TM, LANES), F32)],
        compiler_params=_cparams(1), name="att_merge",
    )(*in_arrays, x2d, _expansion_matrix(ATT_HEADS, ATT_HEAD_DIM), w_out.astype(BF16),
      ln_g.reshape(1, -1), ln_b.reshape(1, -1))


ROUTE_W1, ROUTE_W2, ROUTE_E1, ROUTE_E2, ROUTE_S1, ROUTE_S2 = range(6)
EXPERT_LANE0 = MOE_GROUPS
MOE_TB = 512
RUN_ALIGN = 16
LOCAL_SLOTS = 1536
assert LOCAL_SLOTS >= MOE_TOP_K * MOE_TB + MOE_N_EXPERTS * (RUN_ALIGN - 1)
assert ROUTE_TM == MOE_TB
MAX_PIECES = LOCAL_SLOTS // RUN_ALIGN


def _n_slot_tiles(t):
    n_blocks = t // MOE_TB
    worst = t * MOE_TOP_K + n_blocks * MOE_N_EXPERTS * (RUN_ALIGN - 1) + MOE_N_EXPERTS * (FFN_TM - 1)
    return -(-worst // FFN_TM), (t * MOE_TOP_K) // FFN_TM


def _router_kernel(x_ref, wr_ref, br_ref, route_ref, cnt_ref):
    tm = ROUTE_TM
    logits = _dot_3pass(x_ref[...], wr_ref[...]) + br_ref[...]
    lane_i = lax.broadcasted_iota(jnp.int32, (tm, LANES), 1)
    lane = lane_i.astype(F32)
    big = float(LANES)

    def first_argmax(vals):
        vmax = jnp.max(vals, axis=-1, keepdims=True)
        idx = jnp.min(jnp.where(vals == vmax, lane, big), axis=-1, keepdims=True)
        return vmax, idx

    gl = jnp.where(lane_i < MOE_GROUPS, logits, -jnp.inf)
    gmax, gsel = first_argmax(gl)
    gw = 1.0 / jnp.sum(jnp.exp(gl - gmax), axis=-1, keepdims=True)
    grp_of_lane = jnp.right_shift(lane_i - EXPERT_LANE0, 3).astype(F32)
    in_grp = ((lane_i >= EXPERT_LANE0) & (lane_i < EXPERT_LANE0 + MOE_N_EXPERTS)
              & (grp_of_lane == gsel))
    el = jnp.where(in_grp, logits, -jnp.inf)
    v1, i1 = first_argmax(el)
    v2, i2 = first_argmax(jnp.where(lane == i1, -jnp.inf, el))
    e2 = jnp.exp(v2 - v1)
    w1 = gw / (1.0 + e2)
    w2 = gw * e2 / (1.0 + e2)

    sel1, sel2 = lane == i1, lane == i2
    onehot = jnp.where(sel1 | sel2, 1.0, 0.0)
    r_i = lax.broadcasted_iota(jnp.int32, (tm, tm), 0)
    c_i = lax.broadcasted_iota(jnp.int32, (tm, tm), 1)
    strict = jnp.where(c_i < r_i, 1.0, 0.0).astype(BF16)
    before = jnp.dot(strict, onehot.astype(BF16), preferred_element_type=F32)
    cnt = jnp.sum(onehot, axis=0, keepdims=True)
    run_tiles = jnp.floor((cnt + (RUN_ALIGN - 1)) * (1.0 / RUN_ALIGN))
    e_r = lax.broadcasted_iota(jnp.int32, (LANES, LANES), 0)
    e_c = lax.broadcasted_iota(jnp.int32, (LANES, LANES), 1)
    earlier = jnp.where(e_r < e_c, 1.0, 0.0).astype(BF16)
    run_start = RUN_ALIGN * jnp.dot(jnp.broadcast_to(run_tiles, (SUBLANES, LANES)).astype(BF16),
                                    earlier, preferred_element_type=F32)[0:1, :]
    slot = before + run_start
    slot1 = jnp.sum(jnp.where(sel1, slot, 0.0), axis=-1, keepdims=True)
    slot2 = jnp.sum(jnp.where(sel2, slot, 0.0), axis=-1, keepdims=True)
    cnt_ref[...] = jnp.broadcast_to(cnt, (SUBLANES, LANES))

    out = jnp.zeros((tm, LANES), F32)
    for pos, val in ((ROUTE_W1, w1), (ROUTE_W2, w2), (ROUTE_E1, i1 - EXPERT_LANE0),
                     (ROUTE_E2, i2 - EXPERT_LANE0), (ROUTE_S1, slot1), (ROUTE_S2, slot2)):
        out = jnp.where(lane_i == pos, val, out)
    route_ref[...] = out


def _start_run_copies(tabs, blk, make_copy):
    grow_ref, ntot_ref = tabs
    buf = blk % 2

    def per_piece(j, c):
        g = grow_ref[blk * MAX_PIECES + j]
        make_copy(buf, pl.multiple_of(j * RUN_ALIGN, RUN_ALIGN), pl.multiple_of(g, RUN_ALIGN)).start()
        return c

    lax.fori_loop(0, ntot_ref[blk], per_piece, 0)


def _wait_run_copies(tabs, blk, make_copy):
    buf = blk % 2

    def wait_one(j, c):
        make_copy(buf, 0, 0).wait()
        return c

    lax.fori_loop(0, tabs[1][blk], wait_one, 0)


def _dispatch_kernel(grow_ref, ntot_ref, zrow_ref, zflag_ref,
                     x_ref, route_ref, xs_hbm, loc_ref, zero_ref, sem, zsem):
    i = pl.program_id(0)
    n_zero = zrow_ref.shape[0]

    @pl.when(i == 0)
    def _():
        zero_ref[...] = jnp.zeros_like(zero_ref)

        def zcopy(e):
            start = pl.multiple_of(zrow_ref[e], FFN_TM)
            return pltpu.make_async_copy(zero_ref, xs_hbm.at[pl.ds(start, FFN_TM), :], zsem)

        def zstart(e, carry):
            @pl.when(zflag_ref[e] > 0)
            def _():
                zcopy(e).start()
            return carry

        def zwait(e, carry):
            @pl.when(zflag_ref[e] > 0)
            def _():
                zcopy(e).wait()
            return carry

        lax.fori_loop(0, n_zero, zstart, 0)
        lax.fori_loop(0, n_zero, zwait, 0)

    slots_t = route_ref[...].T
    s1 = slots_t[ROUTE_S1:ROUTE_S1 + 1, :]
    s2 = slots_t[ROUTE_S2:ROUTE_S2 + 1, :]
    xb = x_ref[...].astype(BF16)
    piece = MOE_TB
    buf = i % 2
    for p in range(LOCAL_SLOTS // piece):
        srow = (lax.broadcasted_iota(jnp.int32, (piece, MOE_TB), 0) + p * piece).astype(F32)
        perm = jnp.where((srow == s1) | (srow == s2), 1.0, 0.0).astype(BF16)
        loc_ref[buf, p * piece:(p + 1) * piece, :] = jnp.dot(
            perm, xb, preferred_element_type=F32).astype(BF16)

    def make_copy(b, l, g):
        return pltpu.make_async_copy(loc_ref.at[b, pl.ds(l, RUN_ALIGN), :],
                                     xs_hbm.at[pl.ds(g, RUN_ALIGN), :], sem.at[b])

    tabs = (grow_ref, ntot_ref)
    _start_run_copies(tabs, i, make_copy)

    @pl.when(i > 0)
    def _():
        _wait_run_copies(tabs, i - 1, make_copy)

    @pl.when(i == pl.num_programs(0) - 1)
    def _():
        _wait_run_copies(tabs, i, make_copy)


def _ffn_kernel(te_ref, nused_ref, xs_ref, wg_ref, wu_ref, wd_ref, ys_ref, wgb_ref, wub_ref, wdb_ref):
    i = pl.program_id(0)

    @pl.when((i == 0) | (te_ref[i] != te_ref[jnp.maximum(i - 1, 0)]))
    def _():
        wgb_ref[...] = wg_ref[...].astype(BF16)
        wub_ref[...] = wu_ref[...].astype(BF16)
        wdb_ref[...] = wd_ref[...].astype(BF16)

    @pl.when(i < nused_ref[0])
    def _():
        x = xs_ref[...]
        g = jnp.dot(x, wgb_ref[...], preferred_element_type=F32)
        u = jnp.dot(x, wub_ref[...], preferred_element_type=F32)
        hid = (_silu(g) * u).astype(BF16)
        ys_ref[...] = jnp.dot(hid, wdb_ref[...], preferred_element_type=F32).astype(ys_ref.dtype)

    @pl.when(i >= nused_ref[0])
    def _():
        ys_ref[...] = jnp.zeros_like(ys_ref)


def _combine_kernel(grow_ref, ntot_ref, ys_hbm, route_ref, x_ref,
                    lng_ref, lnb_ref, o_ref, loc_ref, sem):
    i = pl.program_id(0)

    def make_copy(b, l, g):
        return pltpu.make_async_copy(ys_hbm.at[pl.ds(g, RUN_ALIGN), :],
                                     loc_ref.at[b, pl.ds(l, RUN_ALIGN), :], sem.at[b])

    tabs = (grow_ref, ntot_ref)

    @pl.when(i == 0)
    def _():
        loc_ref[...] = jnp.zeros_like(loc_ref)
        _start_run_copies(tabs, i, make_copy)

    @pl.when(i + 1 < pl.num_programs(0))
    def _():
        _start_run_copies(tabs, i + 1, make_copy)

    _wait_run_copies(tabs, i, make_copy)

    route = route_ref[...]
    w1, w2 = route[:, ROUTE_W1:ROUTE_W1 + 1], route[:, ROUTE_W2:ROUTE_W2 + 1]
    s1, s2 = route[:, ROUTE_S1:ROUTE_S1 + 1], route[:, ROUTE_S2:ROUTE_S2 + 1]
    piece = MOE_TB
    buf = i % 2
    f = None
    for p in range(LOCAL_SLOTS // piece):
        scol = (lax.broadcasted_iota(jnp.int32, (MOE_TB, piece), 1) + p * piece).astype(F32)
        comb = (jnp.where(scol == s1, w1, 0.0) + jnp.where(scol == s2, w2, 0.0)).astype(BF16)
        part = jnp.dot(comb, loc_ref[buf, p * piece:(p + 1) * piece, :], preferred_element_type=F32)
        f = part if f is None else f + part
    o_ref[...] = _layernorm(DEEPNORM_ALPHA * x_ref[...] + f, lng_ref[...], lnb_ref[...])


def _moe_layer(x2d, layer, wg, bg, we, be, w_gate, w_up, w_down, ln_g, ln_b):
    t = x2d.shape[0]
    assert t % MOE_TB == 0
    n_blocks = t // MOE_TB
    n_tiles, min_tiles = _n_slot_tiles(t)
    n_slots = n_tiles * FFN_TM

    pad_cols = LANES - MOE_GROUPS - MOE_N_EXPERTS
    wr = jnp.pad(jnp.concatenate([wg, we], axis=1), ((0, 0), (0, pad_cols)))
    br = jnp.pad(jnp.concatenate([bg, be]), (0, pad_cols)).reshape(1, LANES)
    route, counts = pl.pallas_call(
        _router_kernel,
        out_shape=[jax.ShapeDtypeStruct((t, LANES), F32),
                   jax.ShapeDtypeStruct((n_blocks * SUBLANES, LANES), F32)],
        grid=(n_blocks,),
        in_specs=[pl.BlockSpec((ROUTE_TM, D_MODEL), lambda i: (i, 0)),
                  _resident((D_MODEL, LANES)), _resident((1, LANES))],
        out_specs=[pl.BlockSpec((ROUTE_TM, LANES), lambda i: (i, 0)),
                   pl.BlockSpec((SUBLANES, LANES), lambda i: (i, 0))],
        compiler_params=_cparams(1), name="moe_router",
    )(x2d, wr, br)

    cnt = counts.reshape(n_blocks, SUBLANES, LANES)[:, 0, EXPERT_LANE0:EXPERT_LANE0 + MOE_N_EXPERTS]
    cnt = cnt.astype(jnp.int32)
    run = ((cnt + RUN_ALIGN - 1) // RUN_ALIGN) * RUN_ALIGN
    lstart = jnp.cumsum(run, axis=1) - run
    seg_len = jnp.sum(run, axis=0)
    seg_pad = ((seg_len + FFN_TM - 1) // FFN_TM) * FFN_TM
    ends = jnp.cumsum(seg_pad)
    gstart = (ends - seg_pad)[None, :] + jnp.cumsum(run, axis=0) - run
    piece_row = jnp.arange(MAX_PIECES, dtype=jnp.int32)[None, :, None] * RUN_ALIGN
    lend = (lstart + run)[:, None, :]
    owns = (lstart[:, None, :] <= piece_row) & (piece_row < lend)
    grow = jnp.sum(jnp.where(owns, (gstart - lstart)[:, None, :], 0), axis=2) + piece_row[:, :, 0]
    tabs = [grow.astype(jnp.int32).reshape(-1),
            (jnp.sum(run, axis=1) // RUN_ALIGN).astype(jnp.int32)]
    tile_start = jnp.arange(n_tiles, dtype=jnp.int32) * FFN_TM
    tile_expert = jnp.minimum(jnp.sum(ends[None, :] <= tile_start[:, None], axis=1),
                              MOE_N_EXPERTS - 1).astype(jnp.int32)
    n_used = (ends[-1] // FFN_TM).astype(jnp.int32).reshape(1)
    tail_tile = jnp.arange(min_tiles, n_tiles, dtype=jnp.int32)
    zrow = jnp.concatenate([jnp.maximum(ends - FFN_TM, 0), tail_tile * FFN_TM]).astype(jnp.int32)
    zflag = jnp.concatenate([seg_pad > 0, tail_tile >= n_used[0]]).astype(jnp.int32)

    blk_row = lambda w: pl.BlockSpec((MOE_TB, w), lambda i, *_: (i, 0))
    xs = pl.pallas_call(
        _dispatch_kernel,
        out_shape=jax.ShapeDtypeStruct((n_slots, D_MODEL), BF16),
        grid_spec=pltpu.PrefetchScalarGridSpec(
            num_scalar_prefetch=4, grid=(n_blocks,),
            in_specs=[blk_row(D_MODEL), blk_row(LANES)],
            out_specs=pl.BlockSpec(memory_space=pl.ANY),
            scratch_shapes=[pltpu.VMEM((2, LOCAL_SLOTS, D_MODEL), BF16),
                            pltpu.VMEM((FFN_TM, D_MODEL), BF16),
                            pltpu.SemaphoreType.DMA((2,)), pltpu.SemaphoreType.DMA]),
        compiler_params=_cparams(1), name="moe_dispatch",
    )(*tabs, zrow, zflag, x2d, route)

    w_spec = lambda r, c: pl.BlockSpec((None, None, r, c), lambda i, te, nu: (layer, te[i], 0, 0))
    slot_blk = pl.BlockSpec((FFN_TM, D_MODEL), lambda i, te, nu: (i, 0))
    used_blk = pl.BlockSpec((FFN_TM, D_MODEL), lambda i, te, nu: (jnp.minimum(i, nu[0] - 1), 0))
    ys = pl.pallas_call(
        _ffn_kernel,
        out_shape=jax.ShapeDtypeStruct((n_slots, D_MODEL), BF16),
        grid_spec=pltpu.PrefetchScalarGridSpec(
            num_scalar_prefetch=2, grid=(n_tiles,),
            in_specs=[used_blk, w_spec(D_MODEL, MOE_D_EXPERT), w_spec(D_MODEL, MOE_D_EXPERT),
                      w_spec(MOE_D_EXPERT, D_MODEL)],
            out_specs=slot_blk,
            scratch_shapes=[pltpu.VMEM((D_MODEL, MOE_D_EXPERT), BF16),
                            pltpu.VMEM((D_MODEL, MOE_D_EXPERT), BF16),
                            pltpu.VMEM((MOE_D_EXPERT, D_MODEL), BF16)]),
        compiler_params=_cparams(1), name="moe_ffn",
    )(tile_expert, n_used, xs, w_gate, w_up, w_down)

    vec = pl.BlockSpec((1, D_MODEL), lambda i, *_: (0, 0))
    return pl.pallas_call(
        _combine_kernel,
        out_shape=jax.ShapeDtypeStruct((t, D_MODEL), F32),
        grid_spec=pltpu.PrefetchScalarGridSpec(
            num_scalar_prefetch=2, grid=(n_blocks,),
            in_specs=[pl.BlockSpec(memory_space=pl.ANY), blk_row(LANES), blk_row(D_MODEL), vec, vec],
            out_specs=blk_row(D_MODEL),
            scratch_shapes=[pltpu.VMEM((2, LOCAL_SLOTS, D_MODEL), BF16),
                            pltpu.SemaphoreType.DMA((2,))]),
        compiler_params=_cparams(1), name="moe_combine",
    )(*tabs, ys, route, x2d, ln_g.reshape(1, -1), ln_b.reshape(1, -1))


def kernel(x, ssd_w_in, ssd_conv_w, ssd_conv_b, ssd_dt_bias, ssd_a_log, ssd_d, ssd_norm_w, ssd_w_out,
           sc_w_in, sc_conv_w, sc_w_out, att_w_qkv, att_w_out,
           moe_wg, moe_bg, moe_we, moe_be, moe_w_gate, moe_w_up, moe_w_down, ln_g, ln_b):
    bsz, seq, d = x.shape
    assert d == D_MODEL
    h = x.reshape(bsz * seq, d)
    for i in range(DEPTH):
        kind, j = i % N_MIXERS, i // N_MIXERS
        if kind == 0:
            h = _ssd_mixer(h, bsz, seq, ssd_w_in[j], ssd_conv_w[j], ssd_conv_b[j], ssd_dt_bias[j],
                           ssd_a_log[j], ssd_d[j], ssd_norm_w[j], ssd_w_out[j], ln_g[i, 0], ln_b[i, 0])
        elif kind == 1:
            h = _sconv_mixer(h, bsz, seq, sc_w_in[j], sc_conv_w[j], sc_w_out[j], ln_g[i, 0], ln_b[i, 0])
        else:
            h = _attn_mixer(h, bsz, seq, att_w_qkv[j], att_w_out[j], ln_g[i, 0], ln_b[i, 0])
        h = _moe_layer(h, i, moe_wg[i], moe_bg[i], moe_we[i], moe_be[i],
                       moe_w_gate, moe_w_up, moe_w_down, ln_g[i, 1], ln_b[i, 1])
    return h.reshape(bsz, seq, d)
```
